```python
import math
import functools
import jax
import jax.numpy as jnp
from jax import lax
import numpy as np

D_MODEL = 4096
BATCH = 4
SEQ = 2048
DEPTH = 2
DEC_BATCH = 128
DEC_SEQ = 8
PAST_LEN = 16384
PAGE_SIZE = 128

D_FF = 4 * D_MODEL
MIX_WIDTH = D_MODEL
ROPE_THETA = 500000.0
LN_EPS = 1e-5
RMS_EPS = 1e-6
DN_ALPHA = (2 * DEPTH) ** 0.25
DN_BETA = (8 * DEPTH) ** -0.25

A_DIM = 128
A_HEADS = MIX_WIDTH // 2 // A_DIM
A_BLOCK = 256
A_TOPK = 3
A_QCHUNK = 32
A_ROW = 2 * A_DIM
A_SCALE = A_DIM ** -0.5

B_HEADS = MIX_WIDTH // 2 // 128
B_NOPE = 128
B_ROPE = 64
B_VDIM = 128
B_QRANK = 3 * D_MODEL // 16
B_KVRANK = D_MODEL // 16
B_ROPE_THETA = 10000.0
B_QCHUNK = 128
B_ROW = B_KVRANK + B_ROPE
B_SCALE = (B_NOPE + B_ROPE) ** -0.5

C_DIM = 128
C_HEADS = MIX_WIDTH // 2 // C_DIM
C_IDX_HEADS = D_MODEL // 128
C_IDX_DIM = 64
C_TOPK = 256
C_QCHUNK = 128
C_ROW = 2 * C_DIM + C_IDX_DIM
C_SCALE = C_DIM ** -0.5
C_IDX_W_SCALE = (C_IDX_HEADS * C_IDX_DIM) ** -0.5

D_DIM = 64
D_HEADS = MIX_WIDTH // 2 // D_DIM
D_CMP_LEN = 32
D_CMP_STRIDE = 16
D_PHI_HIDDEN = 64
D_SLC_BLOCK = 64
D_SLC_TOPN = 16
D_SLC_LOCAL = 2
D_WINDOW = 512
D_QCHUNK = 128
D_ROW = 4 * D_DIM
D_WIN_ROW = 2 * D_DIM
D_SCALE = D_DIM ** -0.5

EVEN_SPLITS = (A_HEADS * A_DIM, A_DIM, A_DIM, B_QRANK, B_KVRANK, B_ROPE)
ODD_SPLITS = (C_HEADS * C_DIM, C_DIM, C_DIM, C_IDX_HEADS * C_IDX_DIM, C_IDX_DIM, C_IDX_HEADS,
              D_HEADS * D_DIM, 6 * D_DIM, 3 * D_HEADS)

kernel_name = 'hybrid_moba_mla_dsa_nsa_step'


def split_cols(h, sizes):
    cuts = [int(c) for c in np.cumsum(sizes)[:-1]]
    return jnp.split(h, cuts, axis=-1)


def layer_norm(x, g, b):
    xf = x.astype(jnp.float32)
    xc = xf - jnp.mean(xf, axis=-1, keepdims=True)
    var = jnp.mean(xc * xc, axis=-1, keepdims=True)
    return (xc * lax.rsqrt(var + LN_EPS) * g + b).astype(x.dtype)


def rms_norm(x, g):
    xf = x.astype(jnp.float32)
    return (xf * lax.rsqrt(jnp.mean(xf * xf, axis=-1, keepdims=True) + RMS_EPS) * g).astype(x.dtype)


def rope_tables(pos, rot_dim, theta):
    inv = theta ** (-jnp.arange(0, rot_dim, 2, dtype=jnp.float32) / rot_dim)
    ang = pos.astype(jnp.float32)[:, None] * inv[None, :]
    return jnp.cos(ang), jnp.sin(ang)


def apply_rope(x, cos, sin, rot_dim):
    half = rot_dim // 2
    xf = x.astype(jnp.float32)
    x1, x2, rest = xf[..., :half], xf[..., half:rot_dim], xf[..., rot_dim:]
    c, s = cos[:, None, :], sin[:, None, :]
    return jnp.concatenate([x1 * c - x2 * s, x2 * c + x1 * s, rest], axis=-1).astype(x.dtype)


def masked_softmax(s, mask, scale):
    s = jnp.where(mask, s.astype(jnp.float32) * scale, -jnp.inf)
    m = jnp.max(s, axis=-1, keepdims=True)
    e = jnp.exp(s - jnp.where(jnp.isfinite(m), m, 0.0))
    den = jnp.sum(e, axis=-1, keepdims=True)
    return e / jnp.where(den > 0.0, den, 1.0)


def attend(core, fetch, pos, q_arrays, qblock):
    n, s = q_arrays[0].shape[:2]
    nqb = s // qblock
    seq_id = jnp.repeat(jnp.arange(n, dtype=jnp.int32), nqb)
    pos_blk = jnp.tile(pos.reshape(nqb, qblock), (n, 1))
    blocks = [a.reshape((n * nqb, qblock) + a.shape[2:]) for a in q_arrays]

    def step(args):
        b, p = args[0], args[1]
        return core(fetch(b, p), p, *args[2:])

    out = lax.map(step, (seq_id, pos_blk, *blocks))
    return out.reshape((n, s) + out.shape[2:])


def make_fetch(rows, pool, page_table):
    if pool is None:
        return lambda b, p: rows[b]

    def fetch(b, p):
        past = pool[page_table[b]].reshape((-1, pool.shape[-1]))
        return jnp.concatenate([past.astype(rows.dtype), rows[b]], axis=0)
    return fetch


def make_win_fetch(win_rows, win_state, qblock):
    if win_state is None:
        padded = jnp.pad(win_rows, ((0, 0), (D_WINDOW, 0), (0, 0)))

        def fetch(b, p):
            rows = lax.dynamic_slice_in_dim(padded[b], p[0], D_WINDOW + qblock, axis=0)
            return rows, p[0] - D_WINDOW + jnp.arange(D_WINDOW + qblock, dtype=jnp.int32)
        return fetch
    wbuf = win_state.shape[1]
    kpos = PAST_LEN - wbuf + jnp.arange(wbuf + win_rows.shape[1], dtype=jnp.int32)

    def fetch(b, p):
        return jnp.concatenate([win_state[b].astype(win_rows.dtype), win_rows[b]], axis=0), kpos
    return fetch


def moba_core(rows, qpos, q):
    nq, nh = q.shape[:2]
    L = rows.shape[0]
    nb = -(-L // A_BLOCK)
    rows = jnp.pad(rows, ((0, nb * A_BLOCK - L), (0, 0))).reshape(nb, A_BLOCK, A_ROW)
    kb, vb = rows[..., :A_DIM], rows[..., A_DIM:]
    own = qpos // A_BLOCK
    kmean = jnp.mean(kb.astype(jnp.float32), axis=1)
    gate = jnp.einsum('qhd,nd->qhn', q.astype(jnp.float32), kmean)
    past = jnp.arange(nb)[None, None, :] < own[:, None, None]
    _, sel = lax.top_k(jnp.where(past, gate, -jnp.inf), min(A_TOPK, nb))
    sel_ok = sel < own[:, None, None]
    kg, vg = kb[sel], vb[sel]
    ko, vo = kb[own], vb[own]
    s_sel = jnp.einsum('qhd,qhnkd->qhnk', q, kg).reshape(nq, nh, -1)
    s_own = jnp.einsum('qhd,qkd->qhk', q, ko)
    own_pos = own[:, None] * A_BLOCK + jnp.arange(A_BLOCK)[None, :]
    m_sel = jnp.repeat(sel_ok, A_BLOCK, axis=-1)
    m_own = jnp.broadcast_to((own_pos <= qpos[:, None])[:, None, :], s_own.shape)
    p = masked_softmax(jnp.concatenate([s_sel, s_own], axis=-1),
                       jnp.concatenate([m_sel, m_own], axis=-1), A_SCALE).astype(vg.dtype)
    n_sel = s_sel.shape[-1]
    return (jnp.einsum('qhm,qhmd->qhd', p[..., :n_sel], vg.reshape(nq, nh, n_sel, A_DIM))
            + jnp.einsum('qhk,qkd->qhd', p[..., n_sel:], vo))


def mla_core(rows, qpos, q_lat, q_rope):
    ckv, kr = rows[:, :B_KVRANK], rows[:, B_KVRANK:]
    s = jnp.einsum('qhc,lc->qhl', q_lat, ckv) + jnp.einsum('qhr,lr->qhl', q_rope, kr)
    mask = (jnp.arange(rows.shape[0])[None, :] <= qpos[:, None])[:, None, :]
    p = masked_softmax(s, mask, B_SCALE)
    return jnp.einsum('qhl,lc->qhc', p.astype(ckv.dtype), ckv)


def dsa_core(rows, qpos, q, iq, iw):
    L = rows.shape[0]
    k, v, ik = rows[:, :C_DIM], rows[:, C_DIM:2 * C_DIM], rows[:, 2 * C_DIM:]
    adm = jnp.arange(L)[None, :] <= qpos[:, None]
    rel = jax.nn.relu(jnp.einsum('qhd,ld->qhl', iq, ik).astype(jnp.float32))
    score = jnp.einsum('qh,qhl->ql', iw.astype(jnp.float32), rel)
    _, sel = lax.top_k(jnp.where(adm, score, -jnp.inf), min(C_TOPK, L // 4))
    sel_ok = sel <= qpos[:, None]
    s = jnp.einsum('qhd,qnd->qhn', q, k[sel])
    p = masked_softmax(s, sel_ok[:, None, :], C_SCALE)
    return jnp.einsum('qhn,qnd->qhd', p.astype(v.dtype), v[sel])


def nsa_compress(rows, pe, w1, w2):
    r = D_CMP_LEN // D_CMP_STRIDE
    ch = rows.reshape(-1, D_CMP_STRIDE, rows.shape[-1])
    n = ch.shape[0] - r + 1
    blocks = jnp.concatenate([ch[j:j + n] for j in range(r)], axis=1)
    h = jax.nn.silu((blocks + pe).reshape(n, -1) @ w1)
    return h @ w2


def nsa_block_importance(p, nsb):
    rs = D_SLC_BLOCK // D_CMP_STRIDE
    rc = D_CMP_LEN // D_CMP_STRIDE
    pp = jnp.pad(p, ((0, 0), (rc - 1, rc - 1)))
    return sum(pp[:, o:o + rs * nsb:rs] for o in range(rs + rc - 1))


def nsa_core(keys, qpos, q, gates, phi_pe, phi_w1, phi_w2):
    rows, win_rows, win_pos = keys
    nq, nh = q.shape[:2]
    L = rows.shape[0]
    nsb = -(-L // D_SLC_BLOCK)
    rows = jnp.pad(rows, ((0, nsb * D_SLC_BLOCK - L), (0, 0)))
    k_cmp, v_cmp, k_slc, v_slc = jnp.split(rows, 4, axis=-1)
    kc = nsa_compress(k_cmp, phi_pe[0], phi_w1[0], phi_w2[0])
    vc = nsa_compress(v_cmp, phi_pe[1], phi_w1[1], phi_w2[1])
    cmp_end = jnp.arange(kc.shape[0]) * D_CMP_STRIDE + D_CMP_LEN - 1
    cmp_ok = (cmp_end[None, :] <= qpos[:, None])[:, None, :]
    p_cmp = masked_softmax(jnp.einsum('qhd,nd->qhn', q, kc), cmp_ok, D_SCALE)
    o_cmp = jnp.einsum('qhn,nd->qhd', p_cmp.astype(vc.dtype), vc)
    imp = nsa_block_importance(jnp.sum(p_cmp, axis=1), nsb)
    bt = qpos // D_SLC_BLOCK
    jb = jnp.arange(nsb)[None, :]
    forced = (jb == 0) | (jb >= bt[:, None] - (D_SLC_LOCAL - 1))
    imp = jnp.where(jb <= bt[:, None], jnp.where(forced, jnp.inf, imp), -jnp.inf)
    _, sel = lax.top_k(imp, min(D_SLC_TOPN, nsb))
    sel_ok = sel <= bt[:, None]
    kg = k_slc.reshape(nsb, D_SLC_BLOCK, D_DIM)[sel]
    vg = v_slc.reshape(nsb, D_SLC_BLOCK, D_DIM)[sel]
    kpos = sel[..., None] * D_SLC_BLOCK + jnp.arange(D_SLC_BLOCK)
    m_slc = (sel_ok[..., None] & (kpos <= qpos[:, None, None])).reshape(nq, 1, -1)
    s_slc = jnp.einsum('qhd,qnkd->qhnk', q, kg).reshape(nq, nh, -1)
    p_slc = masked_softmax(s_slc, m_slc, D_SCALE)
    o_slc = jnp.einsum('qhm,qmd->qhd', p_slc.astype(vg.dtype), vg.reshape(nq, -1, D_DIM))
    wk, wv = win_rows[:, :D_DIM], win_rows[:, D_DIM:]
    dist = qpos[:, None] - win_pos[None, :]
    m_win = ((dist >= 0) & (dist < D_WINDOW) & (win_pos[None, :] >= 0))[:, None, :]
    p_win = masked_softmax(jnp.einsum('qhd,kd->qhk', q, wk), m_win, D_SCALE)
    o_win = jnp.einsum('qhk,kd->qhd', p_win.astype(wv.dtype), wv)
    return gates[..., 0:1] * o_cmp + gates[..., 1:2] * o_slc + gates[..., 2:3] * o_win


def mixer_even(x, pos, pool_a, pool_b, page_table, w_in, g_cq, g_ckv, w_uq, w_ukv, w_out):
    n, s, _ = x.shape
    aq, ak, av, cq, ckv, kr = split_cols(x @ w_in, EVEN_SPLITS)
    rot = A_DIM // 4
    cos, sin = rope_tables(pos, rot, ROPE_THETA)
    aq = apply_rope(aq.reshape(n, s, A_HEADS, A_DIM), cos, sin, rot)
    ak = apply_rope(ak[:, :, None, :], cos, sin, rot)[:, :, 0]
    a_rows = jnp.concatenate([ak, av], axis=-1)
    o_a = attend(moba_core, make_fetch(a_rows, pool_a, page_table), pos, (aq,), math.gcd(A_QCHUNK, s))
    q = jnp.einsum('nsr,rhe->nshe', rms_norm(cq, g_cq), w_uq)
    bcos, bsin = rope_tables(pos, B_ROPE, B_ROPE_THETA)
    q_rope = apply_rope(q[..., B_NOPE:], bcos, bsin, B_ROPE)
    q_lat = jnp.einsum('nshe,che->nshc', q[..., :B_NOPE], w_ukv[..., :B_NOPE])
    kr = apply_rope(kr[:, :, None, :], bcos, bsin, B_ROPE)[:, :, 0]
    b_rows = jnp.concatenate([rms_norm(ckv, g_ckv), kr], axis=-1)
    o_lat = attend(mla_core, make_fetch(b_rows, pool_b, page_table), pos, (q_lat, q_rope),
                   math.gcd(B_QCHUNK, s))
    o_b = jnp.einsum('nshc,che->nshe', o_lat, w_ukv[..., B_NOPE:])
    o = jnp.concatenate([o_a.reshape(n, s, -1), o_b.reshape(n, s, -1)], axis=-1) @ w_out
    return o, a_rows, b_rows


def mixer_odd(x, pos, pool_c, pool_d, page_table, win_state, w_in, phi_pe, phi_w1, phi_w2, w_out):
    n, s, _ = x.shape
    cq, ck, cv, iq, ik, iw, dq, dkv, dg = split_cols(x @ w_in, ODD_SPLITS)
    rot = C_DIM // 4
    cos, sin = rope_tables(pos, rot, ROPE_THETA)
    cq = apply_rope(cq.reshape(n, s, C_HEADS, C_DIM), cos, sin, rot)
    ck = apply_rope(ck[:, :, None, :], cos, sin, rot)[:, :, 0]
    irot = C_IDX_DIM // 4
    icos, isin = rope_tables(pos, irot, ROPE_THETA)
    iq = apply_rope(iq.reshape(n, s, C_IDX_HEADS, C_IDX_DIM), icos, isin, irot)
    ik = apply_rope(ik[:, :, None, :], icos, isin, irot)[:, :, 0]
    c_rows = jnp.concatenate([ck, cv, ik], axis=-1)
    o_c = attend(dsa_core, make_fetch(c_rows, pool_c, page_table), pos, (cq, iq, iw * C_IDX_W_SCALE),
                 math.gcd(C_QCHUNK, s))
    drot = D_DIM // 4
    dcos, dsin = rope_tables(pos, drot, ROPE_THETA)
    dq = apply_rope(dq.reshape(n, s, D_HEADS, D_DIM), dcos, dsin, drot)
    k_cmp, v_cmp, k_slc, v_slc, k_win, v_win = jnp.split(dkv, 6, axis=-1)
    k_slc = apply_rope(k_slc[:, :, None, :], dcos, dsin, drot)[:, :, 0]
    k_win = apply_rope(k_win[:, :, None, :], dcos, dsin, drot)[:, :, 0]
    d_rows = jnp.concatenate([k_cmp, v_cmp, k_slc, v_slc], axis=-1)
    win_rows = jnp.concatenate([k_win, v_win], axis=-1)
    gates = jax.nn.sigmoid(dg.astype(jnp.float32)).astype(x.dtype).reshape(n, s, D_HEADS, 3)
    qb = math.gcd(D_QCHUNK, s)
    fetch_rows = make_fetch(d_rows, pool_d, page_table)
    fetch_win = make_win_fetch(win_rows, win_state, qb)

    def fetch_d(b, p):
        return (fetch_rows(b, p),) + fetch_win(b, p)

    core = functools.partial(nsa_core, phi_pe=phi_pe, phi_w1=phi_w1, phi_w2=phi_w2)
    o_d = attend(core, fetch_d, pos, (dq, gates), qb)
    o = jnp.concatenate([o_c.reshape(n, s, -1), o_d.reshape(n, s, -1)], axis=-1) @ w_out
    if win_state is None:
        win_new = win_rows[:, -min(D_WINDOW, s):]
    else:
        win_new = jnp.concatenate([win_state.astype(win_rows.dtype), win_rows], axis=1)[:, -win_state.shape[1]:]
    return o, c_rows, d_rows, win_new


def sq_relu_mlp(x, w1, w2):
    h = jax.nn.relu(x @ w1)
    return (h * h) @ w2


def trunk(x, pos, pools, page_table, win_state, weights):
    (w_in_even, b_g_cq, b_g_ckv, b_w_uq, b_w_ukv, w_out_even,
     w_in_odd, d_phi_pe, d_phi_w1, d_phi_w2, w_out_odd,
     mlp_w1, mlp_w2, ln_g, ln_b) = weights
    pool_a, pool_b, pool_c, pool_d = pools
    new_state = ()
    for layer in range(DEPTH):
        if layer % 2 == 0:
            o, a_rows, b_rows = mixer_even(x, pos, pool_a, pool_b, page_table, w_in_even, b_g_cq, b_g_ckv,
                                           b_w_uq, b_w_ukv, w_out_even)
            new_state = new_state + (a_rows, b_rows)
        else:
            o, c_rows, d_rows, win_new = mixer_odd(x, pos, pool_c, pool_d, page_table, win_state, w_in_odd,
                                                   d_phi_pe, d_phi_w1, d_phi_w2, w_out_odd)
            new_state = new_state + (c_rows, d_rows, win_new)
        x = layer_norm(DN_ALPHA * x + o, ln_g[layer, 0], ln_b[layer, 0])
        x = layer_norm(DN_ALPHA * x + sq_relu_mlp(x, mlp_w1[layer], mlp_w2[layer]), ln_g[layer, 1], ln_b[layer, 1])
    return x, new_state


def setup_inputs(seed: int = 0) -> dict:
    key = jax.random.key(seed)
    ks = jax.random.split(key, 32)
    counter = [0]

    def nrm(shape, scale=1.0):
        k = ks[counter[0]]
        counter[0] += 1
        v = jax.random.normal(k, shape, jnp.float32)
        return v if scale == 1.0 else v * scale

    n_pages = PAST_LEN // PAGE_SIZE
    n_used = DEC_BATCH * n_pages
    n_pool = n_used + max(1, n_used // 4)
    w_buf = min(D_WINDOW, PAST_LEN)
    inp = {}
    inp['x_prompt'] = nrm((BATCH, SEQ, D_MODEL))
    inp['x_sample'] = nrm((DEC_BATCH, DEC_SEQ, D_MODEL))
    inp['cache_a_kv'] = nrm((n_pool, PAGE_SIZE, A_ROW))
    inp['cache_b_latent'] = nrm((n_pool, PAGE_SIZE, B_ROW))
    inp['cache_c_kvi'] = nrm((n_pool, PAGE_SIZE, C_ROW))
    inp['cache_d_kv'] = nrm((n_pool, PAGE_SIZE, D_ROW))
    inp['state_d_win'] = nrm((DEC_BATCH, w_buf, D_WIN_ROW))
    k = ks[counter[0]]
    counter[0] += 1
    inp['page_table'] = jax.random.permutation(k, n_pool)[:n_used].reshape(DEC_BATCH, n_pages).astype(jnp.int32)
    inp['w_in_even'] = nrm((D_MODEL, sum(EVEN_SPLITS)), D_MODEL ** -0.5)
    inp['b_g_cq'] = 1.0 + nrm((B_QRANK,), 0.01)
    inp['b_g_ckv'] = 1.0 + nrm((B_KVRANK,), 0.01)
    inp['b_w_uq'] = nrm((B_QRANK, B_HEADS, B_NOPE + B_ROPE), B_QRANK ** -0.5)
    inp['b_w_ukv'] = nrm((B_KVRANK, B_HEADS, B_NOPE + B_VDIM), B_KVRANK ** -0.5)
    inp['w_out_even'] = nrm((MIX_WIDTH, D_MODEL), MIX_WIDTH ** -0.5 * DN_BETA)
    inp['w_in_odd'] = nrm((D_MODEL, sum(ODD_SPLITS)), D_MODEL ** -0.5)
    inp['d_phi_pe'] = nrm((2, D_CMP_LEN, D_DIM), 0.1)
    inp['d_phi_w1'] = nrm((2, D_CMP_LEN * D_DIM, D_PHI_HIDDEN), (D_CMP_LEN * D_DIM) ** -0.5)
    inp['d_phi_w2'] = nrm((2, D_PHI_HIDDEN, D_DIM), D_PHI_HIDDEN ** -0.5)
    inp['w_out_odd'] = nrm((MIX_WIDTH, D_MODEL), MIX_WIDTH ** -0.5 * DN_BETA)
    inp['mlp_w1'] = nrm((DEPTH, D_MODEL, D_FF), D_MODEL ** -0.5)
    inp['mlp_w2'] = nrm((DEPTH, D_FF, D_MODEL), D_FF ** -0.5 * DN_BETA)
    inp['ln_g'] = 1.0 + nrm((DEPTH, 2, D_MODEL), 0.01)
    inp['ln_b'] = nrm((DEPTH, 2, D_MODEL), 0.01)
    return inp


def reference(x_prompt, x_sample, cache_a_kv, cache_b_latent, cache_c_kvi, cache_d_kv, state_d_win, page_table,
              w_in_even, b_g_cq, b_g_ckv, b_w_uq, b_w_ukv, w_out_even,
              w_in_odd, d_phi_pe, d_phi_w1, d_phi_w2, w_out_odd,
              mlp_w1, mlp_w2, ln_g, ln_b):
    weights = (w_in_even, b_g_cq, b_g_ckv, b_w_uq, b_w_ukv, w_out_even,
               w_in_odd, d_phi_pe, d_phi_w1, d_phi_w2, w_out_odd,
               mlp_w1, mlp_w2, ln_g, ln_b)
    pos_p = jnp.arange(x_prompt.shape[1], dtype=jnp.int32)
    pos_s = PAST_LEN + jnp.arange(x_sample.shape[1], dtype=jnp.int32)
    y_prompt, (a_p, b_p, c_p, d_p, w_p) = trunk(x_prompt, pos_p, (None, None, None, None), None, None, weights)
    y_sample, (a_s, b_s, c_s, d_s, w_s) = trunk(x_sample, pos_s,
                                                (cache_a_kv, cache_b_latent, cache_c_kvi, cache_d_kv),
                                                page_table, state_d_win, weights)
    return (y_prompt, y_sample, a_p, a_s, b_p, b_s, c_p, c_s, d_p, d_s, w_p, w_s)
```

```python
import functools
import math

import jax
import jax.numpy as jnp
import numpy as np
from jax import lax
from jax.experimental import pallas as pl
from jax.experimental.pallas import tpu as pltpu

DEPTH = 2
PAGE_SIZE = 128
ROPE_THETA = 500000.0
LN_EPS = 1e-5
RMS_EPS = 1e-6
DN_ALPHA = (2 * DEPTH) ** 0.25

A_DIM = 128
A_HEADS = 16
A_BLOCK = 256
A_ROW = 2 * A_DIM
A_TOPK = 3
A_SCALE = A_DIM ** -0.5

B_HEADS = 16
B_NOPE = 128
B_ROPE = 64
B_VDIM = 128
B_QRANK = 768
B_KVRANK = 256
B_ROPE_THETA = 10000.0
B_SCALE = (B_NOPE + B_ROPE) ** -0.5

C_DIM = 128
C_HEADS = 16
C_IDX_HEADS = 32
C_IDX_DIM = 64
C_TOPK = 256
C_SCALE = C_DIM ** -0.5
C_IDX_W_SCALE = (C_IDX_HEADS * C_IDX_DIM) ** -0.5

D_DIM = 64
D_HEADS = 32
D_CMP_LEN = 32
D_CMP_STRIDE = 16
D_SLC_BLOCK = 64
D_SLC_TOPN = 16
D_SLC_LOCAL = 2
D_WINDOW = 512
D_SCALE = D_DIM ** -0.5

EVEN_SPLITS = (A_HEADS * A_DIM, A_DIM, A_DIM, B_QRANK, B_KVRANK, B_ROPE)
ODD_SPLITS = (C_HEADS * C_DIM, C_DIM, C_DIM, C_IDX_HEADS * C_IDX_DIM, C_IDX_DIM, C_IDX_HEADS,
              D_HEADS * D_DIM, 6 * D_DIM, 3 * D_HEADS)

LANE = 128
VMEM_LIMIT_BYTES = 56 * 1024 * 1024


def _cparams(*sem):
    return pltpu.CompilerParams(dimension_semantics=sem, vmem_limit_bytes=VMEM_LIMIT_BYTES)


def _pick_tile(n, cap, mult):
    best = None
    for t in range(mult, min(n, cap) + 1, mult):
        if n % t == 0:
            best = t
    assert best is not None, (n, cap, mult)
    return best


def _mm_body(x_ref, w_ref, o_ref, acc_ref, *, nk, act):
    k = pl.program_id(2)

    @pl.when(k == 0)
    def _():
        acc_ref[...] = jnp.zeros_like(acc_ref)

    acc_ref[...] += jnp.dot(x_ref[...].astype(jnp.bfloat16), w_ref[...],
                            preferred_element_type=jnp.float32)

    @pl.when(k == nk - 1)
    def _():
        r = acc_ref[...]
        if act == "relu2":
            r = jnp.maximum(r, 0.0)
            r = r * r
        o_ref[...] = r.astype(o_ref.dtype)


def matmul(x, w, *, act=None, out_dtype=jnp.float32):
    m, kdim = x.shape
    n = w.shape[1]
    tm = _pick_tile(m, 1024, 8)
    tn = _pick_tile(n, 1024, LANE)
    tk = _pick_tile(kdim, 1024, LANE)
    nk = kdim // tk
    return pl.pallas_call(
        functools.partial(_mm_body, nk=nk, act=act),
        grid=(m // tm, n // tn, nk),
        in_specs=[pl.BlockSpec((tm, tk), lambda i, j, k: (i, k)),
                  pl.BlockSpec((tk, tn), lambda i, j, k: (k, j))],
        out_specs=pl.BlockSpec((tm, tn), lambda i, j, k: (i, j)),
        out_shape=jax.ShapeDtypeStruct((m, n), out_dtype),
        scratch_shapes=[pltpu.VMEM((tm, tn), jnp.float32)],
        compiler_params=_cparams("parallel", "parallel", "arbitrary"),
        name="matmul",
    )(x, w)


def _ln_body(x_ref, o_ref, g_ref, b_ref, y_ref):
    y = DN_ALPHA * x_ref[...] + o_ref[...]
    mu = jnp.mean(y, axis=-1, keepdims=True)
    yc = y - mu
    var = jnp.mean(yc * yc, axis=-1, keepdims=True)
    y_ref[...] = yc * lax.rsqrt(var + LN_EPS) * g_ref[...] + b_ref[...]


def resid_layer_norm(x, o, g, b):
    m, d = x.shape
    tm = _pick_tile(m, 256, 8)
    row = pl.BlockSpec((tm, d), lambda i: (i, 0))
    vec = pl.BlockSpec((1, d), lambda i: (0, 0))
    return pl.pallas_call(
        _ln_body, grid=(m // tm,), in_specs=[row, row, vec, vec], out_specs=row,
        out_shape=jax.ShapeDtypeStruct((m, d), jnp.float32),
        compiler_params=_cparams("parallel"), name="resid_layer_norm",
    )(x, o, g.reshape(1, d), b.reshape(1, d))


def rms_norm(x, g):
    return x * lax.rsqrt(jnp.mean(x * x, axis=-1, keepdims=True) + RMS_EPS) * g


def rope_tables(pos, rot_dim, theta):
    inv = theta ** (-jnp.arange(0, rot_dim, 2, dtype=jnp.float32) / rot_dim)
    ang = pos.astype(jnp.float32)[:, None] * inv[None, :]
    return jnp.cos(ang), jnp.sin(ang)


def apply_rope(x, cos, sin, rot_dim):
    half = rot_dim // 2
    x1, x2, rest = x[..., :half], x[..., half:rot_dim], x[..., rot_dim:]
    c, s = cos[:, None, :], sin[:, None, :]
    return jnp.concatenate([x1 * c - x2 * s, x2 * c + x1 * s, rest], axis=-1)


def masked_softmax(s, mask, scale):
    s = jnp.where(mask, s * scale, -jnp.inf)
    m = jnp.max(s, axis=-1, keepdims=True)
    e = jnp.exp(s - jnp.where(jnp.isfinite(m), m, 0.0))
    den = jnp.sum(e, axis=-1, keepdims=True)
    return e / jnp.where(den > 0.0, den, 1.0)


def attend(core, fetch, pos, q_arrays, qblock):
    n, s = q_arrays[0].shape[:2]
    nqb = s // qblock
    seq_id = jnp.repeat(jnp.arange(n, dtype=jnp.int32), nqb)
    pos_blk = jnp.tile(pos.reshape(nqb, qblock), (n, 1))
    blocks = [a.reshape((n * nqb, qblock) + a.shape[2:]) for a in q_arrays]

    def step(args):
        b, p = args[0], args[1]
        return core(fetch(b, p), p, *args[2:])

    out = lax.map(step, (seq_id, pos_blk, *blocks))
    return out.reshape((n, s) + out.shape[2:])


def make_fetch(rows, pool, page_table):
    if pool is None:
        return lambda b, p: rows[b]

    def fetch(b, p):
        past = pool[page_table[b]].reshape((-1, pool.shape[-1]))
        return jnp.concatenate([past, rows[b]], axis=0)
    return fetch


def make_win_fetch(win_rows, win_state, qblock, past_len):
    if win_state is None:
        padded = jnp.pad(win_rows, ((0, 0), (D_WINDOW, 0), (0, 0)))

        def fetch(b, p):
            rows = lax.dynamic_slice_in_dim(padded[b], p[0], D_WINDOW + qblock, axis=0)
            return rows, p[0] - D_WINDOW + jnp.arange(D_WINDOW + qblock, dtype=jnp.int32)
        return fetch
    wbuf = win_state.shape[1]
    kpos = past_len - wbuf + jnp.arange(wbuf + win_rows.shape[1], dtype=jnp.int32)

    def fetch(b, p):
        return jnp.concatenate([win_state[b], win_rows[b]], axis=0), kpos
    return fetch


def moba_core(rows, qpos, q):
    nq, nh = q.shape[:2]
    L = rows.shape[0]
    nb = -(-L // A_BLOCK)
    rows = jnp.pad(rows, ((0, nb * A_BLOCK - L), (0, 0))).reshape(nb, A_BLOCK, A_ROW)
    kb, vb = rows[..., :A_DIM], rows[..., A_DIM:]
    own = qpos // A_BLOCK
    kmean = jnp.mean(kb, axis=1)
    gate = jnp.einsum('qhd,nd->qhn', q, kmean)
    past = jnp.arange(nb)[None, None, :] < own[:, None, None]
    _, sel = lax.top_k(jnp.where(past, gate, -jnp.inf), min(A_TOPK, nb))
    sel_ok = sel < own[:, None, None]
    kg, vg = kb[sel], vb[sel]
    ko, vo = kb[own], vb[own]
    s_sel = jnp.einsum('qhd,qhnkd->qhnk', q, kg).reshape(nq, nh, -1)
    s_own = jnp.einsum('qhd,qkd->qhk', q, ko)
    own_pos = own[:, None] * A_BLOCK + jnp.arange(A_BLOCK)[None, :]
    m_sel = jnp.repeat(sel_ok, A_BLOCK, axis=-1)
    m_own = jnp.broadcast_to((own_pos <= qpos[:, None])[:, None, :], s_own.shape)
    p = masked_softmax(jnp.concatenate([s_sel, s_own], axis=-1),
                       jnp.concatenate([m_sel, m_own], axis=-1), A_SCALE)
    n_sel = s_sel.shape[-1]
    return (jnp.einsum('qhm,qhmd->qhd', p[..., :n_sel], vg.reshape(nq, nh, n_sel, A_DIM))
            + jnp.einsum('qhk,qkd->qhd', p[..., n_sel:], vo))


def mla_core(rows, qpos, q_lat, q_rope):
    ckv, kr = rows[:, :B_KVRANK], rows[:, B_KVRANK:]
    s = jnp.einsum('qhc,lc->qhl', q_lat, ckv) + jnp.einsum('qhr,lr->qhl', q_rope, kr)
    mask = (jnp.arange(rows.shape[0])[None, :] <= qpos[:, None])[:, None, :]
    p = masked_softmax(s, mask, B_SCALE)
    return jnp.einsum('qhl,lc->qhc', p, ckv)


def dsa_core(rows, qpos, q, iq, iw):
    L = rows.shape[0]
    k, v, ik = rows[:, :C_DIM], rows[:, C_DIM:2 * C_DIM], rows[:, 2 * C_DIM:]
    adm = jnp.arange(L)[None, :] <= qpos[:, None]
    rel = jax.nn.relu(jnp.einsum('qhd,ld->qhl', iq, ik))
    score = jnp.einsum('qh,qhl->ql', iw, rel)
    _, sel = lax.top_k(jnp.where(adm, score, -jnp.inf), min(C_TOPK, L // 4))
    sel_ok = sel <= qpos[:, None]
    s = jnp.einsum('qhd,qnd->qhn', q, k[sel])
    p = masked_softmax(s, sel_ok[:, None, :], C_SCALE)
    return jnp.einsum('qhn,qnd->qhd', p, v[sel])


def nsa_compress(rows, pe, w1, w2):
    r = D_CMP_LEN // D_CMP_STRIDE
    ch = rows.reshape(-1, D_CMP_STRIDE, rows.shape[-1])
    n = ch.shape[0] - r + 1
    blocks = jnp.concatenate([ch[j:j + n] for j in range(r)], axis=1)
    h = jax.nn.silu((blocks + pe).reshape(n, -1) @ w1)
    return h @ w2


def nsa_block_importance(p, nsb):
    rs = D_SLC_BLOCK // D_CMP_STRIDE
    rc = D_CMP_LEN // D_CMP_STRIDE
    pp = jnp.pad(p, ((0, 0), (rc - 1, rc - 1)))
    return sum(pp[:, o:o + rs * nsb:rs] for o in range(rs + rc - 1))


def nsa_core(keys, qpos, q, gates, phi_pe, phi_w1, phi_w2):
    rows, win_rows, win_pos = keys
    nq, nh = q.shape[:2]
    L = rows.shape[0]
    nsb = -(-L // D_SLC_BLOCK)
    rows = jnp.pad(rows, ((0, nsb * D_SLC_BLOCK - L), (0, 0)))
    k_cmp, v_cmp, k_slc, v_slc = jnp.split(rows, 4, axis=-1)
    kc = nsa_compress(k_cmp, phi_pe[0], phi_w1[0], phi_w2[0])
    vc = nsa_compress(v_cmp, phi_pe[1], phi_w1[1], phi_w2[1])
    cmp_end = jnp.arange(kc.shape[0]) * D_CMP_STRIDE + D_CMP_LEN - 1
    cmp_ok = (cmp_end[None, :] <= qpos[:, None])[:, None, :]
    p_cmp = masked_softmax(jnp.einsum('qhd,nd->qhn', q, kc), cmp_ok, D_SCALE)
    o_cmp = jnp.einsum('qhn,nd->qhd', p_cmp, vc)
    imp = nsa_block_importance(jnp.sum(p_cmp, axis=1), nsb)
    bt = qpos // D_SLC_BLOCK
    jb = jnp.arange(nsb)[None, :]
    forced = (jb == 0) | (jb >= bt[:, None] - (D_SLC_LOCAL - 1))
    imp = jnp.where(jb <= bt[:, None], jnp.where(forced, jnp.inf, imp), -jnp.inf)
    _, sel = lax.top_k(imp, min(D_SLC_TOPN, nsb))
    sel_ok = sel <= bt[:, None]
    kg = k_slc.reshape(nsb, D_SLC_BLOCK, D_DIM)[sel]
    vg = v_slc.reshape(nsb, D_SLC_BLOCK, D_DIM)[sel]
    kpos = sel[..., None] * D_SLC_BLOCK + jnp.arange(D_SLC_BLOCK)
    m_slc = (sel_ok[..., None] & (kpos <= qpos[:, None, None])).reshape(nq, 1, -1)
    s_slc = jnp.einsum('qhd,qnkd->qhnk', q, kg).reshape(nq, nh, -1)
    p_slc = masked_softmax(s_slc, m_slc, D_SCALE)
    o_slc = jnp.einsum('qhm,qmd->qhd', p_slc, vg.reshape(nq, -1, D_DIM))
    wk, wv = win_rows[:, :D_DIM], win_rows[:, D_DIM:]
    dist = qpos[:, None] - win_pos[None, :]
    m_win = ((dist >= 0) & (dist < D_WINDOW) & (win_pos[None, :] >= 0))[:, None, :]
    p_win = masked_softmax(jnp.einsum('qhd,kd->qhk', q, wk), m_win, D_SCALE)
    o_win = jnp.einsum('qhk,kd->qhd', p_win, wv)
    return gates[..., 0:1] * o_cmp + gates[..., 1:2] * o_slc + gates[..., 2:3] * o_win


def _pad_cols(w, mult):
    n = w.shape[1]
    npad = -(-n // mult) * mult
    return jnp.pad(w, ((0, 0), (0, npad - n))) if npad != n else w


def _split(h, sizes):
    cuts = [int(c) for c in np.cumsum(sizes)[:-1]]
    return jnp.split(h[..., :sum(sizes)], cuts, axis=-1)


def _even_attn(h, pos, pool_a, pool_b, page_table, g_cq, g_ckv, w_uq, w_ukv):
    n, s, _ = h.shape
    aq, ak, av, cq, ckv, kr = _split(h, EVEN_SPLITS)
    rot = A_DIM // 4
    cos, sin = rope_tables(pos, rot, ROPE_THETA)
    aq = apply_rope(aq.reshape(n, s, A_HEADS, A_DIM), cos, sin, rot)
    ak = apply_rope(ak[:, :, None, :], cos, sin, rot)[:, :, 0]
    a_rows = jnp.concatenate([ak, av], axis=-1)
    o_a = attend(moba_core, make_fetch(a_rows, pool_a, page_table), pos, (aq,), math.gcd(32, s))
    q = jnp.einsum('nsr,rhe->nshe', rms_norm(cq, g_cq), w_uq)
    bcos, bsin = rope_tables(pos, B_ROPE, B_ROPE_THETA)
    q_rope = apply_rope(q[..., B_NOPE:], bcos, bsin, B_ROPE)
    q_lat = jnp.einsum('nshe,che->nshc', q[..., :B_NOPE], w_ukv[..., :B_NOPE])
    kr = apply_rope(kr[:, :, None, :], bcos, bsin, B_ROPE)[:, :, 0]
    b_rows = jnp.concatenate([rms_norm(ckv, g_ckv), kr], axis=-1)
    o_lat = attend(mla_core, make_fetch(b_rows, pool_b, page_table), pos, (q_lat, q_rope), math.gcd(128, s))
    o_b = jnp.einsum('nshc,che->nshe', o_lat, w_ukv[..., B_NOPE:])
    o = jnp.concatenate([o_a.reshape(n, s, -1), o_b.reshape(n, s, -1)], axis=-1)
    return o, a_rows, b_rows


def _odd_attn(h, pos, pool_c, pool_d, page_table, win_state, past_len, phi_pe, phi_w1, phi_w2):
    n, s, _ = h.shape
    cq, ck, cv, iq, ik, iw, dq, dkv, dg = _split(h, ODD_SPLITS)
    rot = C_DIM // 4
    cos, sin = rope_tables(pos, rot, ROPE_THETA)
    cq = apply_rope(cq.reshape(n, s, C_HEADS, C_DIM), cos, sin, rot)
    ck = apply_rope(ck[:, :, None, :], cos, sin, rot)[:, :, 0]
    irot = C_IDX_DIM // 4
    icos, isin = rope_tables(pos, irot, ROPE_THETA)
    iq = apply_rope(iq.reshape(n, s, C_IDX_HEADS, C_IDX_DIM), icos, isin, irot)
    ik = apply_rope(ik[:, :, None, :], icos, isin, irot)[:, :, 0]
    c_rows = jnp.concatenate([ck, cv, ik], axis=-1)
    o_c = attend(dsa_core, make_fetch(c_rows, pool_c, page_table), pos, (cq, iq, iw * C_IDX_W_SCALE),
                 math.gcd(128, s))
    drot = D_DIM // 4
    dcos, dsin = rope_tables(pos, drot, ROPE_THETA)
    dq = apply_rope(dq.reshape(n, s, D_HEADS, D_DIM), dcos, dsin, drot)
    k_cmp, v_cmp, k_slc, v_slc, k_win, v_win = jnp.split(dkv, 6, axis=-1)
    k_slc = apply_rope(k_slc[:, :, None, :], dcos, dsin, drot)[:, :, 0]
    k_win = apply_rope(k_win[:, :, None, :], dcos, dsin, drot)[:, :, 0]
    d_rows = jnp.concatenate([k_cmp, v_cmp, k_slc, v_slc], axis=-1)
    win_rows = jnp.concatenate([k_win, v_win], axis=-1)
    gates = jax.nn.sigmoid(dg).reshape(n, s, D_HEADS, 3)
    qb = math.gcd(128, s)
    fetch_rows = make_fetch(d_rows, pool_d, page_table)
    fetch_win = make_win_fetch(win_rows, win_state, qb, past_len)

    def fetch_d(b, p):
        return (fetch_rows(b, p),) + fetch_win(b, p)

    core = functools.partial(nsa_core, phi_pe=phi_pe, phi_w1=phi_w1, phi_w2=phi_w2)
    o_d = attend(core, fetch_d, pos, (dq, gates), qb)
    o = jnp.concatenate([o_c.reshape(n, s, -1), o_d.reshape(n, s, -1)], axis=-1)
    if win_state is None:
        win_new = win_rows[:, -min(D_WINDOW, s):]
    else:
        win_new = jnp.concatenate([win_state, win_rows], axis=1)[:, -win_state.shape[1]:]
    return o, c_rows, d_rows, win_new


def kernel(x_prompt, x_sample, cache_a_kv, cache_b_latent, cache_c_kvi, cache_d_kv, state_d_win, page_table,
           w_in_even, b_g_cq, b_g_ckv, b_w_uq, b_w_ukv, w_out_even,
           w_in_odd, d_phi_pe, d_phi_w1, d_phi_w2, w_out_odd,
           mlp_w1, mlp_w2, ln_g, ln_b):
    nb, sp, d = x_prompt.shape
    ns, ss, _ = x_sample.shape
    tp = nb * sp
    past_len = page_table.shape[1] * PAGE_SIZE
    pos_p = jnp.arange(sp, dtype=jnp.int32)
    pos_s = past_len + jnp.arange(ss, dtype=jnp.int32)
    bf = jnp.bfloat16
    x = jnp.concatenate([x_prompt.reshape(tp, d), x_sample.reshape(ns * ss, d)], axis=0)

    def groups(h):
        return h[:tp].reshape(nb, sp, -1), h[tp:].reshape(ns, ss, -1)

    def merge(op, os_):
        return jnp.concatenate([op.reshape(tp, -1), os_.reshape(ns * ss, -1)], axis=0)

    def post(x, o, w_out, layer):
        x = resid_layer_norm(x, matmul(o, w_out.astype(bf)), ln_g[layer, 0], ln_b[layer, 0])
        hm = matmul(x, mlp_w1[layer].astype(bf), act="relu2", out_dtype=bf)
        return resid_layer_norm(x, matmul(hm, mlp_w2[layer].astype(bf)), ln_g[layer, 1], ln_b[layer, 1])

    hp, hs = groups(matmul(x, _pad_cols(w_in_even, LANE).astype(bf)))
    o_p, a_p, b_p = _even_attn(hp, pos_p, None, None, None, b_g_cq, b_g_ckv, b_w_uq, b_w_ukv)
    o_s, a_s, b_s = _even_attn(hs, pos_s, cache_a_kv, cache_b_latent, page_table, b_g_cq, b_g_ckv, b_w_uq, b_w_ukv)
    x = post(x, merge(o_p, o_s), w_out_even, 0)
    hp, hs = groups(matmul(x, _pad_cols(w_in_odd, LANE).astype(bf)))
    o_p, c_p, d_p, w_p = _odd_attn(hp, pos_p, None, None, None, None, past_len, d_phi_pe, d_phi_w1, d_phi_w2)
    o_s, c_s, d_s, w_s = _odd_attn(hs, pos_s, cache_c_kvi, cache_d_kv, page_table, state_d_win, past_len,
                                   d_phi_pe, d_phi_w1, d_phi_w2)
    x = post(x, merge(o_p, o_s), w_out_odd, 1)
    y_p, y_s = groups(x)
    return (y_p, y_s, a_p, a_s, b_p, b_s, c_p, c_s, d_p, d_s, w_p, w_s)
```

```python
import functools

import jax
import jax.numpy as jnp
import numpy as np
from jax import lax
from jax.experimental import pallas as pl
from jax.experimental.pallas import tpu as pltpu

DEPTH = 2
PAGE_SIZE = 128
ROPE_THETA = 500000.0
LN_EPS = 1e-5
RMS_EPS = 1e-6
DN_ALPHA = (2 * DEPTH) ** 0.25

A_DIM = 128
A_HEADS = 16
A_BLOCK = 256
A_ROW = 2 * A_DIM
A_TOPK = 3
A_SCALE = A_DIM ** -0.5

B_HEADS = 16
B_NOPE = 128
B_ROPE = 64
B_VDIM = 128
B_QRANK = 768
B_KVRANK = 256
B_ROPE_THETA = 10000.0
B_SCALE = (B_NOPE + B_ROPE) ** -0.5

C_DIM = 128
C_HEADS = 16
C_IDX_HEADS = 32
C_IDX_DIM = 64
C_TOPK = 256
C_SCALE = C_DIM ** -0.5
C_IDX_W_SCALE = (C_IDX_HEADS * C_IDX_DIM) ** -0.5

D_DIM = 64
D_HEADS = 32
D_CMP_LEN = 32
D_CMP_STRIDE = 16
D_SLC_BLOCK = 64
D_SLC_TOPN = 16
D_SLC_LOCAL = 2
D_WINDOW = 512
D_SCALE = D_DIM ** -0.5

EVEN_SPLITS = (A_HEADS * A_DIM, A_DIM, A_DIM, B_QRANK, B_KVRANK, B_ROPE)
ODD_SPLITS = (C_HEADS * C_DIM, C_DIM, C_DIM, C_IDX_HEADS * C_IDX_DIM, C_IDX_DIM, C_IDX_HEADS,
              D_HEADS * D_DIM, 6 * D_DIM, 3 * D_HEADS)

LANE = 128
VMEM_LIMIT_BYTES = 56 * 1024 * 1024


def _cparams(*sem):
    return pltpu.CompilerParams(dimension_semantics=sem, vmem_limit_bytes=VMEM_LIMIT_BYTES)


def _pick_tile(n, cap, mult):
    best = None
    for t in range(mult, min(n, cap) + 1, mult):
        if n % t == 0:
            best = t
    assert best is not None, (n, cap, mult)
    return best


def _mm_body(x_ref, w_ref, o_ref, acc_ref, *, nk, act):
    k = pl.program_id(2)

    @pl.when(k == 0)
    def _():
        acc_ref[...] = jnp.zeros_like(acc_ref)

    acc_ref[...] += jnp.dot(x_ref[...].astype(jnp.bfloat16), w_ref[...],
                            preferred_element_type=jnp.float32)

    @pl.when(k == nk - 1)
    def _():
        r = acc_ref[...]
        if act == "relu2":
            r = jnp.maximum(r, 0.0)
            r = r * r
        o_ref[...] = r.astype(o_ref.dtype)


def matmul(x, w, *, act=None, out_dtype=jnp.float32):
    m, kdim = x.shape
    n = w.shape[1]
    tm = _pick_tile(m, 1024, 8)
    tn = _pick_tile(n, 1024, LANE)
    tk = _pick_tile(kdim, 1024, LANE)
    nk = kdim // tk
    return pl.pallas_call(
        functools.partial(_mm_body, nk=nk, act=act),
        grid=(m // tm, n // tn, nk),
        in_specs=[pl.BlockSpec((tm, tk), lambda i, j, k: (i, k)),
                  pl.BlockSpec((tk, tn), lambda i, j, k: (k, j))],
        out_specs=pl.BlockSpec((tm, tn), lambda i, j, k: (i, j)),
        out_shape=jax.ShapeDtypeStruct((m, n), out_dtype),
        scratch_shapes=[pltpu.VMEM((tm, tn), jnp.float32)],
        compiler_params=_cparams("parallel", "parallel", "arbitrary"),
        name="matmul",
    )(x, w)


def _ln_body(x_ref, o_ref, g_ref, b_ref, y_ref):
    y = DN_ALPHA * x_ref[...] + o_ref[...]
    mu = jnp.mean(y, axis=-1, keepdims=True)
    yc = y - mu
    var = jnp.mean(yc * yc, axis=-1, keepdims=True)
    y_ref[...] = yc * lax.rsqrt(var + LN_EPS) * g_ref[...] + b_ref[...]


def resid_layer_norm(x, o, g, b):
    m, d = x.shape
    tm = _pick_tile(m, 256, 8)
    row = pl.BlockSpec((tm, d), lambda i: (i, 0))
    vec = pl.BlockSpec((1, d), lambda i: (0, 0))
    return pl.pallas_call(
        _ln_body, grid=(m // tm,), in_specs=[row, row, vec, vec], out_specs=row,
        out_shape=jax.ShapeDtypeStruct((m, d), jnp.float32),
        compiler_params=_cparams("parallel"), name="resid_layer_norm",
    )(x, o, g.reshape(1, d), b.reshape(1, d))


NEG = -1e30
BF = jnp.bfloat16
F32 = jnp.float32


def _dot(a, b):
    return jnp.dot(a, b, preferred_element_type=F32)


def _dot_nt(a, b):
    return lax.dot_general(a, b, (((1,), (1,)), ((), ())), preferred_element_type=F32)


def _rope(x, c, sm, sp, half):
    return x * c + pltpu.roll(x, LANE - half, 1) * sm + pltpu.roll(x, half, 1) * sp


def _add_head_mask(s, maskf, nh):
    tq, tk = maskf.shape
    return (s.reshape(nh, tq, tk) + maskf[None]).reshape(nh * tq, tk)


def _softmax_init(m_ref, l_ref, acc_ref):
    m_ref[...] = jnp.full(m_ref.shape, NEG, F32)
    l_ref[...] = jnp.zeros(l_ref.shape, F32)
    acc_ref[...] = jnp.zeros(acc_ref.shape, F32)


def _softmax_step(s, v, m_ref, l_ref, acc_ref):
    m_prev = m_ref[...]
    m_new = jnp.maximum(m_prev, jnp.max(s, axis=-1, keepdims=True))
    alpha = jnp.exp(m_prev - m_new)
    p = jnp.where(s > 0.5 * NEG, jnp.exp(s - m_new), 0.0)
    l_ref[...] = alpha * l_ref[...] + jnp.sum(p, axis=-1, keepdims=True)
    acc_ref[...] = alpha * acc_ref[...] + _dot(p.astype(BF), v)
    m_ref[...] = m_new


def _softmax_finish(l_ref, acc_ref):
    l = l_ref[...]
    return acc_ref[...] / jnp.where(l > 0.0, l, 1.0)


def _top_lanes(g, avail, k):
    lane = lax.broadcasted_iota(jnp.int32, g.shape, 1)
    sel = jnp.zeros(g.shape, jnp.bool_)
    for _ in range(k):
        gm = jnp.where(avail, g, -jnp.inf)
        mx = jnp.max(gm, axis=-1, keepdims=True)
        idx = jnp.min(jnp.where(avail & (gm == mx), lane, jnp.int32(2 ** 30)), axis=-1, keepdims=True)
        pick = lane == idx
        sel = sel | pick
        avail = avail & jnp.logical_not(pick)
    return sel


def _rope_tables(pos, rot_dim, theta, width):
    half = rot_dim // 2
    inv = theta ** (-jnp.arange(0, rot_dim, 2, dtype=jnp.float32) / rot_dim)
    ang = pos.astype(jnp.float32)[:, None] * inv[None, :]
    c, s = jnp.cos(ang), jnp.sin(ang)
    t = pos.shape[0]
    one = jnp.ones((t, width - rot_dim), F32)
    z_half = jnp.zeros((t, half), F32)
    z_rest = jnp.zeros((t, width - rot_dim), F32)
    rep = LANE // width
    cc = jnp.tile(jnp.concatenate([c, c, one], axis=1), (1, rep))
    sm = jnp.tile(jnp.concatenate([-s, z_half, z_rest], axis=1), (1, rep))
    sp = jnp.tile(jnp.concatenate([z_half, s, z_rest], axis=1), (1, rep))
    return cc, sm, sp


def _tok_spec(tm, w):
    return pl.BlockSpec((tm, w), lambda i: (i, 0))


def _full_spec(shape):
    nd = len(shape)
    return pl.BlockSpec(shape, lambda *_: (0,) * nd)


_E_AK, _E_AV, _E_CQ, _E_CKV, _E_KR = 2048, 2176, 2304, 3072, 3328


def _prep_even_body(h_ref, ca, sma, spa, cb, smb, spb, gq_ref, gkv_ref,
                    aq_ref, arows_ref, cqn_ref, brows_ref):
    c, sm, sp = ca[...], sma[...], spa[...]
    ha = A_DIM // 8
    for h in range(A_HEADS):
        aq_ref[:, h * LANE:(h + 1) * LANE] = _rope(h_ref[:, h * LANE:(h + 1) * LANE], c, sm, sp, ha)
    arows_ref[:, 0:A_DIM] = _rope(h_ref[:, _E_AK:_E_AK + A_DIM], c, sm, sp, ha)
    arows_ref[:, A_DIM:2 * A_DIM] = h_ref[:, _E_AV:_E_AV + A_DIM]
    cq = h_ref[:, _E_CQ:_E_CQ + B_QRANK]
    cqn = cq * lax.rsqrt(jnp.mean(cq * cq, axis=-1, keepdims=True) + RMS_EPS) * gq_ref[...]
    cqn_ref[...] = cqn.astype(cqn_ref.dtype)
    ckv = h_ref[:, _E_CKV:_E_CKV + B_KVRANK]
    brows_ref[:, 0:B_KVRANK] = ckv * lax.rsqrt(jnp.mean(ckv * ckv, axis=-1, keepdims=True) + RMS_EPS) * gkv_ref[...]
    kr = _rope(h_ref[:, _E_KR:_E_KR + LANE], cb[...], smb[...], spb[...], B_ROPE // 2)
    brows_ref[:, B_KVRANK:B_KVRANK + B_ROPE] = kr[:, 0:B_ROPE]


def prep_even(h, tab_a, tab_b, g_cq, g_ckv):
    t, w = h.shape
    tm = _pick_tile(t, 256, 8)
    tab = _tok_spec(tm, LANE)
    return pl.pallas_call(
        _prep_even_body, grid=(t // tm,),
        in_specs=[_tok_spec(tm, w), tab, tab, tab, tab, tab, tab,
                  _full_spec((1, B_QRANK)), _full_spec((1, B_KVRANK))],
        out_specs=[_tok_spec(tm, A_HEADS * A_DIM), _tok_spec(tm, A_ROW),
                   _tok_spec(tm, B_QRANK), _tok_spec(tm, B_KVRANK + B_ROPE)],
        out_shape=[jax.ShapeDtypeStruct((t, A_HEADS * A_DIM), F32),
                   jax.ShapeDtypeStruct((t, A_ROW), F32),
                   jax.ShapeDtypeStruct((t, B_QRANK), BF),
                   jax.ShapeDtypeStruct((t, B_KVRANK + B_ROPE), F32)],
        compiler_params=_cparams("parallel"), name="prep_even",
    )(h, *tab_a, *tab_b, g_cq.reshape(1, -1), g_ckv.reshape(1, -1))


def _mla_qprep_body(q_ref, wuk_ref, cb, smb, spb, ql_ref, qr_ref):
    for h in range(B_HEADS):
        qn = q_ref[:, h * B_NOPE:(h + 1) * B_NOPE].astype(BF)
        ql_ref[:, h * B_KVRANK:(h + 1) * B_KVRANK] = _dot(qn, wuk_ref[h])
    c, sm, sp = cb[...], smb[...], spb[...]
    lane = lax.broadcasted_iota(jnp.int32, c.shape, 1)
    base = B_HEADS * B_NOPE
    for j in range(B_HEADS // 2):
        r = _rope(q_ref[:, base + j * LANE:base + (j + 1) * LANE], c, sm, sp, B_ROPE // 2)
        qr_ref[:, (2 * j) * LANE:(2 * j + 1) * LANE] = jnp.where(lane < B_ROPE, r, 0.0)
        qr_ref[:, (2 * j + 1) * LANE:(2 * j + 2) * LANE] = jnp.where(lane < B_ROPE, pltpu.roll(r, B_ROPE, 1), 0.0)


def mla_qprep(q, wuk_t, tab_b):
    t, w = q.shape
    tm = _pick_tile(t, 256, 8)
    tab = _tok_spec(tm, LANE)
    return pl.pallas_call(
        _mla_qprep_body, grid=(t // tm,),
        in_specs=[_tok_spec(tm, w), _full_spec(wuk_t.shape), tab, tab, tab],
        out_specs=[_tok_spec(tm, B_HEADS * B_KVRANK), _tok_spec(tm, B_HEADS * LANE)],
        out_shape=[jax.ShapeDtypeStruct((t, B_HEADS * B_KVRANK), F32),
                   jax.ShapeDtypeStruct((t, B_HEADS * LANE), F32)],
        compiler_params=_cparams("parallel"), name="mla_qprep",
    )(q, wuk_t, *tab_b)


def _mla_oproj_body(ol_ref, wuv_ref, o_ref):
    for h in range(B_HEADS):
        ol = ol_ref[:, h * B_KVRANK:(h + 1) * B_KVRANK].astype(BF)
        o_ref[:, h * B_VDIM:(h + 1) * B_VDIM] = _dot(ol, wuv_ref[h]).astype(o_ref.dtype)


def mla_oproj(o_lat, wuv):
    t, w = o_lat.shape
    tm = _pick_tile(t, 256, 8)
    return pl.pallas_call(
        _mla_oproj_body, grid=(t // tm,),
        in_specs=[_tok_spec(tm, w), _full_spec(wuv.shape)],
        out_specs=_tok_spec(tm, B_HEADS * B_VDIM),
        out_shape=jax.ShapeDtypeStruct((t, B_HEADS * B_VDIM), BF),
        compiler_params=_cparams("parallel"), name="mla_oproj",
    )(o_lat, wuv)


def _pages_per_step(n_pages):
    return _pick_tile(n_pages, 16, 1)


def _page_specs(g_pages, width, col_block):
    def spec(g):
        return pl.BlockSpec((None, PAGE_SIZE, width),
                            lambda b, s, pt: (pt[b, s * g_pages + g], 0, col_block))
    return [spec(g) for g in range(g_pages)]


def _cat_pages(pages, lo, hi):
    return jnp.concatenate([p[:, lo:hi] for p in pages], axis=0)


def _moba_p_body(aq_ref, rows_ref, o_ref, qs, m_ref, l_ref, acc_ref, *, tq, sp):
    nh = A_HEADS
    qi = pl.program_id(1)
    q0 = qi * tq
    own = q0 // A_BLOCK
    nblk = sp // A_BLOCK
    for h in range(nh):
        qs[h * tq:(h + 1) * tq, :] = aq_ref[:, h * A_DIM:(h + 1) * A_DIM].astype(BF)
    q = qs[...]
    km = [jnp.mean(rows_ref[n * A_BLOCK:(n + 1) * A_BLOCK, 0:A_DIM], axis=0, keepdims=True) for n in range(nblk)]
    km = jnp.concatenate(km + [jnp.zeros((LANE - nblk, A_DIM), F32)], axis=0)
    gate = _dot_nt(q, km.astype(BF))
    lane = lax.broadcasted_iota(jnp.int32, gate.shape, 1)
    selm = jnp.where(_top_lanes(gate, lane < own, A_TOPK), 0.0, NEG)
    _softmax_init(m_ref, l_ref, acc_ref)
    for n in range(nblk):
        @pl.when(n <= own)
        def _():
            k = rows_ref[n * A_BLOCK:(n + 1) * A_BLOCK, 0:A_DIM].astype(BF)
            v = rows_ref[n * A_BLOCK:(n + 1) * A_BLOCK, A_DIM:2 * A_DIM].astype(BF)
            s = _dot_nt(q, k) * A_SCALE
            qpos = q0 + lax.broadcasted_iota(jnp.int32, (tq, A_BLOCK), 0)
            kpos = n * A_BLOCK + lax.broadcasted_iota(jnp.int32, (tq, A_BLOCK), 1)
            causal = jnp.where(kpos <= qpos, 0.0, NEG)
            s_own = _add_head_mask(s, causal, nh)
            s_past = s + selm[:, n:n + 1]
            _softmax_step(jnp.where(n == own, s_own, s_past), v, m_ref, l_ref, acc_ref)
    res = _softmax_finish(l_ref, acc_ref)
    for h in range(nh):
        o_ref[:, h * A_DIM:(h + 1) * A_DIM] = res[h * tq:(h + 1) * tq, :].astype(o_ref.dtype)


def moba_prompt(aq, a_rows, nb, sp):
    tq = _pick_tile(sp, 128, 8)
    nq = sp // tq
    m = A_HEADS * tq
    return pl.pallas_call(
        functools.partial(_moba_p_body, tq=tq, sp=sp), grid=(nb, nq),
        in_specs=[pl.BlockSpec((tq, A_HEADS * A_DIM), lambda b, i: (b * nq + i, 0)),
                  pl.BlockSpec((sp, A_ROW), lambda b, i: (b, 0))],
        out_specs=pl.BlockSpec((tq, A_HEADS * A_DIM), lambda b, i: (b * nq + i, 0)),
        out_shape=jax.ShapeDtypeStruct((nb * sp, A_HEADS * A_DIM), BF),
        scratch_shapes=[pltpu.VMEM((m, A_DIM), BF), pltpu.VMEM((m, 1), F32), pltpu.VMEM((m, 1), F32),
                        pltpu.VMEM((m, A_DIM), F32)],
        compiler_params=_cparams("parallel", "arbitrary"), name="moba_prompt",
    )(aq, a_rows)


def _moba_s_body(pt_ref, aq_ref, new_ref, *rest, g_pages, steps, ss):
    pages = rest[:g_pages]
    o_ref, qs, gs, ms, ls, accs = rest[g_pages:]
    nh = A_HEADS
    m_rows = nh * ss
    step = pl.program_id(1)
    bps = g_pages * PAGE_SIZE // A_BLOCK
    ppb = A_BLOCK // PAGE_SIZE

    @pl.when(step == 0)
    def _():
        for h in range(nh):
            qs[h * ss:(h + 1) * ss, :] = aq_ref[:, h * A_DIM:(h + 1) * A_DIM]
        gs[...] = jnp.full(gs.shape, NEG, F32)
        ms[...] = jnp.full(ms.shape, NEG, F32)
        ls[...] = jnp.zeros(ls.shape, F32)

    q = qs[...].astype(BF)
    lane = lax.broadcasted_iota(jnp.int32, (m_rows, LANE), 1)
    for i in range(bps):
        n = step * bps + i
        blk = pages[i * ppb:(i + 1) * ppb]
        k = _cat_pages(blk, 0, A_DIM).astype(BF)
        v = _cat_pages(blk, A_DIM, 2 * A_DIM).astype(BF)
        s = _dot_nt(q, k) * A_SCALE
        m_n = jnp.max(s, axis=-1, keepdims=True)
        p = jnp.exp(s - m_n)
        here = lane == n
        gs[...] = jnp.where(here, jnp.mean(s, axis=-1, keepdims=True), gs[...])
        ms[...] = jnp.where(here, m_n, ms[...])
        ls[...] = jnp.where(here, jnp.sum(p, axis=-1, keepdims=True), ls[...])
        accs[n] = _dot(p.astype(BF), v)

    @pl.when(step == steps - 1)
    def _():
        nblk = steps * bps
        sel = _top_lanes(gs[...], lane < nblk, A_TOPK)
        kn = new_ref[:, 0:A_DIM].astype(BF)
        vn = new_ref[:, A_DIM:2 * A_DIM].astype(BF)
        qrow = lax.broadcasted_iota(jnp.int32, (ss, ss), 0)
        kcol = lax.broadcasted_iota(jnp.int32, (ss, ss), 1)
        s_own = _add_head_mask(_dot_nt(q, kn) * A_SCALE, jnp.where(kcol <= qrow, 0.0, NEG), nh)
        m_own = jnp.max(s_own, axis=-1, keepdims=True)
        p_own = jnp.where(s_own > 0.5 * NEG, jnp.exp(s_own - m_own), 0.0)
        m_fin = jnp.maximum(jnp.max(jnp.where(sel, ms[...], NEG), axis=-1, keepdims=True), m_own)
        w = jnp.where(sel, jnp.exp(ms[...] - m_fin), 0.0)
        w_own = jnp.exp(m_own - m_fin)
        l_fin = jnp.sum(w * ls[...], axis=-1, keepdims=True) + w_own * jnp.sum(p_own, axis=-1, keepdims=True)
        acc = w_own * _dot(p_own.astype(BF), vn)
        for n in range(nblk):
            acc = acc + w[:, n:n + 1] * accs[n]
        res = acc / l_fin
        for h in range(nh):
            o_ref[:, h * A_DIM:(h + 1) * A_DIM] = res[h * ss:(h + 1) * ss, :].astype(o_ref.dtype)


def moba_sample(aq, a_rows, pool, page_table, tp, ns, ss):
    n_pages = page_table.shape[1]
    g_pages = _pages_per_step(n_pages)
    steps = n_pages // g_pages
    assert g_pages % (A_BLOCK // PAGE_SIZE) == 0 and n_pages * PAGE_SIZE // A_BLOCK <= LANE
    m = A_HEADS * ss
    tok = lambda w: pl.BlockSpec((ss, w), lambda b, s, pt: (tp // ss + b, 0))
    grid_spec = pltpu.PrefetchScalarGridSpec(
        num_scalar_prefetch=1, grid=(ns, steps),
        in_specs=[tok(A_HEADS * A_DIM), tok(A_ROW)] + _page_specs(g_pages, A_ROW, 0),
        out_specs=pl.BlockSpec((ss, A_HEADS * A_DIM), lambda b, s, pt: (b, 0)),
        scratch_shapes=[pltpu.VMEM((m, A_DIM), F32), pltpu.VMEM((m, LANE), F32), pltpu.VMEM((m, LANE), F32),
                        pltpu.VMEM((m, LANE), F32),
                        pltpu.VMEM((n_pages * PAGE_SIZE // A_BLOCK, m, A_DIM), F32)])
    return pl.pallas_call(
        functools.partial(_moba_s_body, g_pages=g_pages, steps=steps, ss=ss), grid_spec=grid_spec,
        out_shape=jax.ShapeDtypeStruct((ns * ss, A_HEADS * A_DIM), BF),
        compiler_params=_cparams("parallel", "arbitrary"), name="moba_sample",
    )(page_table, aq, a_rows, *([pool] * g_pages))


def _mla_scores(qls, qrs, blk):
    ckv = blk[:, 0:B_KVRANK].astype(BF)
    kr = blk[:, B_KVRANK:B_KVRANK + B_ROPE].astype(BF)
    return (_dot_nt(qls, ckv) + _dot_nt(qrs, kr)) * B_SCALE, ckv


def _mla_p_body(ql_ref, qr_ref, rows_ref, o_ref, qls, qrs, m_ref, l_ref, acc_ref, *, tq, tk):
    nh = B_HEADS
    qi = pl.program_id(1)
    q0 = qi * tq
    for h in range(nh):
        qls[h * tq:(h + 1) * tq, :] = ql_ref[:, h * B_KVRANK:(h + 1) * B_KVRANK].astype(BF)
        qrs[h * tq:(h + 1) * tq, :] = qr_ref[:, h * LANE:h * LANE + B_ROPE].astype(BF)
    _softmax_init(m_ref, l_ref, acc_ref)

    def body(j, carry):
        k0 = pl.multiple_of(j * tk, tk)
        s, ckv = _mla_scores(qls[...], qrs[...], rows_ref[pl.ds(k0, tk), :])
        qpos = q0 + lax.broadcasted_iota(jnp.int32, (tq, tk), 0)
        kpos = k0 + lax.broadcasted_iota(jnp.int32, (tq, tk), 1)
        s = _add_head_mask(s, jnp.where(kpos <= qpos, 0.0, NEG), nh)
        _softmax_step(s, ckv, m_ref, l_ref, acc_ref)
        return carry

    lax.fori_loop(0, (q0 + tq + tk - 1) // tk, body, 0)
    res = _softmax_finish(l_ref, acc_ref)
    for h in range(nh):
        o_ref[:, h * B_KVRANK:(h + 1) * B_KVRANK] = res[h * tq:(h + 1) * tq, :]


def mla_prompt(q_lat, q_rope, b_rows, nb, sp):
    tq = _pick_tile(sp, 128, 8)
    tk = _pick_tile(sp, 256, 8)
    nq = sp // tq
    m = B_HEADS * tq
    tok = lambda w: pl.BlockSpec((tq, w), lambda b, i: (b * nq + i, 0))
    return pl.pallas_call(
        functools.partial(_mla_p_body, tq=tq, tk=tk), grid=(nb, nq),
        in_specs=[tok(B_HEADS * B_KVRANK), tok(B_HEADS * LANE),
                  pl.BlockSpec((sp, B_KVRANK + B_ROPE), lambda b, i: (b, 0))],
        out_specs=tok(B_HEADS * B_KVRANK),
        out_shape=jax.ShapeDtypeStruct((nb * sp, B_HEADS * B_KVRANK), F32),
        scratch_shapes=[pltpu.VMEM((m, B_KVRANK), BF), pltpu.VMEM((m, B_ROPE), BF),
                        pltpu.VMEM((m, 1), F32), pltpu.VMEM((m, 1), F32), pltpu.VMEM((m, B_KVRANK), F32)],
        compiler_params=_cparams("parallel", "arbitrary"), name="mla_prompt",
    )(q_lat, q_rope, b_rows)


def _mla_s_body(pt_ref, ql_ref, qr_ref, new_ref, *rest, g_pages, steps, ss):
    pages = rest[:g_pages]
    o_ref, qls, qrs, m_ref, l_ref, acc_ref = rest[g_pages:]
    nh = B_HEADS
    step = pl.program_id(1)

    @pl.when(step == 0)
    def _():
        for h in range(nh):
            qls[h * ss:(h + 1) * ss, :] = ql_ref[:, h * B_KVRANK:(h + 1) * B_KVRANK]
            qrs[h * ss:(h + 1) * ss, :] = qr_ref[:, h * LANE:(h + 1) * LANE]
        _softmax_init(m_ref, l_ref, acc_ref)

    ql = qls[...].astype(BF)
    qr = qrs[:, 0:B_ROPE].astype(BF)
    s, ckv = _mla_scores(ql, qr, _cat_pages(pages, 0, B_KVRANK + B_ROPE))
    _softmax_step(s, ckv, m_ref, l_ref, acc_ref)

    @pl.when(step == steps - 1)
    def _():
        s_new, ckv_new = _mla_scores(ql, qr, new_ref[...])
        qrow = lax.broadcasted_iota(jnp.int32, (ss, ss), 0)
        kcol = lax.broadcasted_iota(jnp.int32, (ss, ss), 1)
        _softmax_step(_add_head_mask(s_new, jnp.where(kcol <= qrow, 0.0, NEG), nh), ckv_new, m_ref, l_ref, acc_ref)
        res = _softmax_finish(l_ref, acc_ref)
        for h in range(nh):
            o_ref[:, h * B_KVRANK:(h + 1) * B_KVRANK] = res[h * ss:(h + 1) * ss, :]


def mla_sample(q_lat, q_rope, b_rows, pool, page_table, tp, ns, ss):
    n_pages = page_table.shape[1]
    g_pages = _pages_per_step(n_pages)
    steps = n_pages // g_pages
    m = B_HEADS * ss
    row_w = B_KVRANK + B_ROPE
    tok = lambda w: pl.BlockSpec((ss, w), lambda b, s, pt: (tp // ss + b, 0))
    grid_spec = pltpu.PrefetchScalarGridSpec(
        num_scalar_prefetch=1, grid=(ns, steps),
        in_specs=[tok(B_HEADS * B_KVRANK), tok(B_HEADS * LANE), tok(row_w)] + _page_specs(g_pages, row_w, 0),
        out_specs=pl.BlockSpec((ss, B_HEADS * B_KVRANK), lambda b, s, pt: (b, 0)),
        scratch_shapes=[pltpu.VMEM((m, B_KVRANK), F32), pltpu.VMEM((m, LANE), F32),
                        pltpu.VMEM((m, 1), F32), pltpu.VMEM((m, 1), F32), pltpu.VMEM((m, B_KVRANK), F32)])
    return pl.pallas_call(
        functools.partial(_mla_s_body, g_pages=g_pages, steps=steps, ss=ss), grid_spec=grid_spec,
        out_shape=jax.ShapeDtypeStruct((ns * ss, B_HEADS * B_KVRANK), F32),
        compiler_params=_cparams("parallel", "arbitrary"), name="mla_sample",
    )(page_table, q_lat, q_rope, b_rows, *([pool] * g_pages))


def even_mixers(h, pos, tp, nb, sp, ns, ss, pool_a, pool_b, page_table, g_cq, g_ckv, w_uq, w_ukv):
    tab_a = _rope_tables(pos, A_DIM // 4, ROPE_THETA, A_DIM)
    tab_b = _rope_tables(pos, B_ROPE, B_ROPE_THETA, B_ROPE)
    aq, a_rows, cqn, b_rows = prep_even(h, tab_a, tab_b, g_cq, g_ckv)
    w_uq_cols = jnp.concatenate([w_uq[:, :, :B_NOPE].reshape(B_QRANK, -1),
                                 w_uq[:, :, B_NOPE:].reshape(B_QRANK, -1)], axis=1).astype(BF)
    q = matmul(cqn, w_uq_cols)
    wuk_t = jnp.transpose(w_ukv[:, :, :B_NOPE], (1, 2, 0)).astype(BF)
    wuv = jnp.transpose(w_ukv[:, :, B_NOPE:], (1, 0, 2)).astype(BF)
    q_lat, q_rope = mla_qprep(q, wuk_t, tab_b)
    o_a = jnp.concatenate([moba_prompt(aq, a_rows, nb, sp),
                           moba_sample(aq, a_rows, pool_a, page_table, tp, ns, ss)], axis=0)
    o_lat = jnp.concatenate([mla_prompt(q_lat, q_rope, b_rows, nb, sp),
                             mla_sample(q_lat, q_rope, b_rows, pool_b, page_table, tp, ns, ss)], axis=0)
    o_b = mla_oproj(o_lat, wuv)
    return jnp.concatenate([o_a, o_b], axis=1), a_rows, b_rows


_O_CK, _O_CV, _O_IQ, _O_IK, _O_IW, _O_DQ, _O_DKV, _O_DG, _O_END = 2048, 2176, 2304, 4352, 4480, 4608, 6656, 7040, 7168
INT_MIN = -2 ** 31


def _odd_weight_cols(w):
    cuts = [int(c) for c in np.cumsum(ODD_SPLITS)[:-1]]
    cq, ck, cv, iq, ik, iw, dq, dkv, dg = jnp.split(w, cuts, axis=1)
    dg = dg.reshape(-1, D_HEADS, 3).transpose(0, 2, 1).reshape(-1, 3 * D_HEADS)
    pad = lambda a, n: jnp.pad(a, ((0, 0), (0, n - a.shape[1])))
    return jnp.concatenate([cq, ck, cv, iq, pad(ik, LANE), pad(iw, LANE), dq, dkv, pad(dg, LANE)], axis=1)


def _split_heads_64(r, lane):
    return jnp.where(lane < 64, r, 0.0), jnp.where(lane < 64, pltpu.roll(r, 64, 1), 0.0)


def _prep_odd_body(h_ref, ca, sma, spa, ci, smi, spi, cq_ref, crows_ref, iq_ref, iw_ref, dq_ref, drows_ref,
                   win_ref, gate_ref):
    c, sm, sp = ca[...], sma[...], spa[...]
    ha = C_DIM // 8
    for h in range(C_HEADS):
        cq_ref[:, h * LANE:(h + 1) * LANE] = _rope(h_ref[:, h * LANE:(h + 1) * LANE], c, sm, sp, ha)
    crows_ref[:, 0:C_DIM] = _rope(h_ref[:, _O_CK:_O_CK + C_DIM], c, sm, sp, ha)
    crows_ref[:, C_DIM:2 * C_DIM] = h_ref[:, _O_CV:_O_CV + C_DIM]
    c, sm, sp = ci[...], smi[...], spi[...]
    hi = C_IDX_DIM // 8
    lane = lax.broadcasted_iota(jnp.int32, c.shape, 1)
    ik = _rope(h_ref[:, _O_IK:_O_IK + LANE], c, sm, sp, hi)
    crows_ref[:, 2 * C_DIM:2 * C_DIM + C_IDX_DIM] = ik[:, 0:C_IDX_DIM]
    for j in range(C_IDX_HEADS // 2):
        a, b = _split_heads_64(_rope(h_ref[:, _O_IQ + j * LANE:_O_IQ + (j + 1) * LANE], c, sm, sp, hi), lane)
        iq_ref[:, (2 * j) * LANE:(2 * j + 1) * LANE] = a
        iq_ref[:, (2 * j + 1) * LANE:(2 * j + 2) * LANE] = b
    iw_ref[...] = h_ref[:, _O_IW:_O_IW + LANE] * C_IDX_W_SCALE
    for j in range(D_HEADS // 2):
        a, b = _split_heads_64(_rope(h_ref[:, _O_DQ + j * LANE:_O_DQ + (j + 1) * LANE], c, sm, sp, hi), lane)
        dq_ref[:, (2 * j) * LANE:(2 * j + 1) * LANE] = a
        dq_ref[:, (2 * j + 1) * LANE:(2 * j + 2) * LANE] = b
    left = lane < D_DIM
    c1, sm1, sp1 = jnp.where(left, c, 1.0), jnp.where(left, sm, 0.0), jnp.where(left, sp, 0.0)
    drows_ref[:, 0:LANE] = h_ref[:, _O_DKV:_O_DKV + LANE]
    drows_ref[:, LANE:2 * LANE] = _rope(h_ref[:, _O_DKV + LANE:_O_DKV + 2 * LANE], c1, sm1, sp1, hi)
    win_ref[...] = _rope(h_ref[:, _O_DKV + 2 * LANE:_O_DKV + 3 * LANE], c1, sm1, sp1, hi)
    gate_ref[...] = jax.nn.sigmoid(h_ref[:, _O_DG:_O_DG + LANE])


def prep_odd(h, tab_a, tab_i):
    t, w = h.shape
    tm = _pick_tile(t, 256, 8)
    tab = _tok_spec(tm, LANE)
    widths = [C_HEADS * C_DIM, 2 * C_DIM + C_IDX_DIM, C_IDX_HEADS * LANE, LANE, D_HEADS * LANE, 4 * D_DIM, LANE, LANE]
    return pl.pallas_call(
        _prep_odd_body, grid=(t // tm,),
        in_specs=[_tok_spec(tm, w), tab, tab, tab, tab, tab, tab],
        out_specs=[_tok_spec(tm, wd) for wd in widths],
        out_shape=[jax.ShapeDtypeStruct((t, wd), F32) for wd in widths],
        compiler_params=_cparams("parallel"), name="prep_odd",
    )(h, *tab_a, *tab_i)


def _sort_key(score, adm):
    b = lax.bitcast_convert_type(score + 0.0, jnp.int32)
    key = b ^ ((b >> 31) & jnp.int32(0x7FFFFFFF))
    return jnp.where(adm, key, jnp.int32(INT_MIN))


def _kth_largest(count_ge, shape, k):
    def body(it, ans):
        cand = ans + jnp.left_shift(jnp.int32(1), 31 - it)
        return jnp.where(count_ge(cand) >= k, cand, ans)
    return lax.fori_loop(0, 32, body, jnp.full(shape, INT_MIN, jnp.int32))


def _index_scores(iq_heads, iw, ikt):
    score = None
    for h in range(C_IDX_HEADS):
        rel = jnp.maximum(_dot_nt(iq_heads(h), ikt), 0.0)
        term = iw[:, h:h + 1] * rel
        score = term if score is None else score + term
    return score


def _tie_prefix(tie):
    r = lax.broadcasted_iota(jnp.int32, (LANE, LANE), 0)
    c = lax.broadcasted_iota(jnp.int32, (LANE, LANE), 1)
    upper = jnp.where(r < c, 1.0, 0.0).astype(BF)
    return _dot(jnp.where(tie, 1.0, 0.0).astype(BF), upper)


def _dsa_p_body(cq_ref, iq_ref, iw_ref, rows_ref, o_ref, qs, sck, m_ref, l_ref, acc_ref, *, tq, tk, sp, kk):
    nh = C_HEADS
    qi = pl.program_id(1)
    q0 = qi * tq
    ntile = sp // tk
    for h in range(nh):
        qs[h * tq:(h + 1) * tq, :] = cq_ref[:, h * C_DIM:(h + 1) * C_DIM].astype(BF)
    iw = iw_ref[...]
    qpos = q0 + lax.broadcasted_iota(jnp.int32, (tq, tk), 0)
    lane_k = lax.broadcasted_iota(jnp.int32, (tq, tk), 1)
    for j in range(ntile):
        @pl.when(j * tk < q0 + tq)
        def _():
            ikt = rows_ref[j * tk:(j + 1) * tk, 2 * C_DIM:2 * C_DIM + C_IDX_DIM].astype(BF)
            score = _index_scores(lambda h: iq_ref[:, h * LANE:h * LANE + C_IDX_DIM].astype(BF), iw, ikt)
            sck[:, j * tk:(j + 1) * tk] = _sort_key(score, j * tk + lane_k <= qpos)

        @pl.when(j * tk >= q0 + tq)
        def _():
            sck[:, j * tk:(j + 1) * tk] = jnp.full((tq, tk), INT_MIN, jnp.int32)

    def count_ge(t):
        return jnp.sum(jnp.where(sck[...] >= t, 1.0, 0.0), axis=-1, keepdims=True)

    thr = _kth_largest(count_ge, (tq, 1), float(kk))
    n_gt = jnp.sum(jnp.where(sck[...] > thr, 1.0, 0.0), axis=-1, keepdims=True)
    need = float(kk) - n_gt
    excess = (count_ge(thr) - n_gt > need) & (thr > INT_MIN)

    @pl.when(jnp.max(jnp.where(excess, 1.0, 0.0)) > 0.0)
    def _():
        carry = jnp.zeros((tq, 1), F32)
        for c in range(sp // LANE):
            keys = sck[:, c * LANE:(c + 1) * LANE]
            tie = keys == thr
            drop = excess & tie & (carry + _tie_prefix(tie) >= need)
            sck[:, c * LANE:(c + 1) * LANE] = jnp.where(drop, jnp.int32(INT_MIN), keys)
            carry = carry + jnp.sum(jnp.where(tie, 1.0, 0.0), axis=-1, keepdims=True)

    thr = jnp.maximum(thr, INT_MIN + 1)
    _softmax_init(m_ref, l_ref, acc_ref)
    q = qs[...]
    for j in range(ntile):
        @pl.when(j * tk < q0 + tq)
        def _():
            k = rows_ref[j * tk:(j + 1) * tk, 0:C_DIM].astype(BF)
            v = rows_ref[j * tk:(j + 1) * tk, C_DIM:2 * C_DIM].astype(BF)
            maskf = jnp.where(sck[:, j * tk:(j + 1) * tk] >= thr, 0.0, NEG)
            _softmax_step(_add_head_mask(_dot_nt(q, k) * C_SCALE, maskf, nh), v, m_ref, l_ref, acc_ref)
    res = _softmax_finish(l_ref, acc_ref)
    for h in range(nh):
        o_ref[:, h * C_DIM:(h + 1) * C_DIM] = res[h * tq:(h + 1) * tq, :].astype(o_ref.dtype)


def dsa_prompt(cq, iq, iw, c_rows, nb, sp):
    tq = _pick_tile(sp, 128, 8)
    tk = _pick_tile(sp, 512, LANE)
    nq = sp // tq
    m = C_HEADS * tq
    tok = lambda w: pl.BlockSpec((tq, w), lambda b, i: (b * nq + i, 0))
    return pl.pallas_call(
        functools.partial(_dsa_p_body, tq=tq, tk=tk, sp=sp, kk=min(C_TOPK, sp // 4)), grid=(nb, nq),
        in_specs=[tok(C_HEADS * C_DIM), tok(C_IDX_HEADS * LANE), tok(LANE),
                  pl.BlockSpec((sp, 2 * C_DIM + C_IDX_DIM), lambda b, i: (b, 0))],
        out_specs=tok(C_HEADS * C_DIM),
        out_shape=jax.ShapeDtypeStruct((nb * sp, C_HEADS * C_DIM), BF),
        scratch_shapes=[pltpu.VMEM((m, C_DIM), BF), pltpu.VMEM((tq, sp), jnp.int32),
                        pltpu.VMEM((m, 1), F32), pltpu.VMEM((m, 1), F32), pltpu.VMEM((m, C_DIM), F32)],
        compiler_params=_cparams("parallel", "arbitrary"), name="dsa_prompt",
    )(cq, iq, iw, c_rows)


def _dsa_si_body(pt_ref, iq_ref, iw_ref, new_ref, *rest, g_pages, steps, ss, kk):
    pages = rest[:g_pages]
    mask_ref, iqs, sc = rest[g_pages:]
    step = pl.program_id(1)
    w = g_pages * PAGE_SIZE
    nhi = C_IDX_HEADS

    @pl.when(step == 0)
    def _():
        for h in range(nhi):
            iqs[h * ss:(h + 1) * ss, :] = iq_ref[:, h * LANE:(h + 1) * LANE]

    iw = iw_ref[...]
    heads = lambda h: iqs[h * ss:(h + 1) * ss, 0:C_IDX_DIM].astype(BF)
    ikt = _cat_pages(pages, 0, C_IDX_DIM).astype(BF)
    sc[step] = _sort_key(_index_scores(heads, iw, ikt), jnp.full((ss, w), True))

    @pl.when(step == steps - 1)
    def _():
        ikn = new_ref[:, 2 * C_DIM:2 * C_DIM + C_IDX_DIM].astype(BF)
        s_new = _index_scores(heads, iw, ikn)
        qrow = lax.broadcasted_iota(jnp.int32, (ss, ss), 0)
        kcol = lax.broadcasted_iota(jnp.int32, (ss, ss), 1)
        sc[steps] = jnp.full((ss, w), INT_MIN, jnp.int32)
        sc[steps, :, 0:ss] = _sort_key(s_new, kcol <= qrow)

        def count_ge(t):
            return jnp.sum(jnp.sum(jnp.where(sc[...] >= t[None], 1.0, 0.0), axis=0), axis=-1, keepdims=True)

        thr = _kth_largest(count_ge, (ss, 1), float(kk))
        n_gt = jnp.sum(jnp.sum(jnp.where(sc[...] > thr[None], 1.0, 0.0), axis=0), axis=-1, keepdims=True)
        need = float(kk) - n_gt
        excess = (count_ge(thr) - n_gt > need) & (thr > INT_MIN)

        @pl.when(jnp.max(jnp.where(excess, 1.0, 0.0)) > 0.0)
        def _():
            def slot(s, carry):
                for c in range(w // LANE):
                    keys = sc[s, :, c * LANE:(c + 1) * LANE]
                    tie = keys == thr
                    drop = excess & tie & (carry + _tie_prefix(tie) >= need)
                    sc[s, :, c * LANE:(c + 1) * LANE] = jnp.where(drop, jnp.int32(INT_MIN), keys)
                    carry = carry + jnp.sum(jnp.where(tie, 1.0, 0.0), axis=-1, keepdims=True)
                return carry
            lax.fori_loop(0, steps + 1, slot, jnp.zeros((ss, 1), F32))

        thr2 = jnp.maximum(thr, INT_MIN + 1)
        mask_ref[...] = jnp.where(sc[...] >= thr2[None], 0.0, NEG)


def _dsa_sa_body(pt_ref, cq_ref, new_ref, mask_ref, mask_new_ref, *rest, g_pages, steps, ss):
    pages = rest[:g_pages]
    o_ref, qs, m_ref, l_ref, acc_ref = rest[g_pages:]
    nh = C_HEADS
    step = pl.program_id(1)

    @pl.when(step == 0)
    def _():
        for h in range(nh):
            qs[h * ss:(h + 1) * ss, :] = cq_ref[:, h * C_DIM:(h + 1) * C_DIM]
        _softmax_init(m_ref, l_ref, acc_ref)

    q = qs[...].astype(BF)
    k = _cat_pages(pages, 0, C_DIM).astype(BF)
    v = _cat_pages(pages, C_DIM, 2 * C_DIM).astype(BF)
    _softmax_step(_add_head_mask(_dot_nt(q, k) * C_SCALE, mask_ref[...], nh), v, m_ref, l_ref, acc_ref)

    @pl.when(step == steps - 1)
    def _():
        kn = new_ref[:, 0:C_DIM].astype(BF)
        vn = new_ref[:, C_DIM:2 * C_DIM].astype(BF)
        s_new = _add_head_mask(_dot_nt(q, kn) * C_SCALE, mask_new_ref[:, 0:ss], nh)
        _softmax_step(s_new, vn, m_ref, l_ref, acc_ref)
        res = _softmax_finish(l_ref, acc_ref)
        for h in range(nh):
            o_ref[:, h * C_DIM:(h + 1) * C_DIM] = res[h * ss:(h + 1) * ss, :].astype(o_ref.dtype)


def dsa_sample(cq, iq, iw, c_rows, pool, page_table, tp, ns, ss):
    n_pages = page_table.shape[1]
    g_pages = _pages_per_step(n_pages)
    steps = n_pages // g_pages
    w = g_pages * PAGE_SIZE
    row_w = 2 * C_DIM + C_IDX_DIM
    kk = min(C_TOPK, (n_pages * PAGE_SIZE + ss) // 4)
    tok = lambda wd: pl.BlockSpec((ss, wd), lambda b, s, pt: (tp // ss + b, 0))
    mask = pl.pallas_call(
        functools.partial(_dsa_si_body, g_pages=g_pages, steps=steps, ss=ss, kk=kk),
        grid_spec=pltpu.PrefetchScalarGridSpec(
            num_scalar_prefetch=1, grid=(ns, steps),
            in_specs=[tok(C_IDX_HEADS * LANE), tok(LANE), tok(row_w)]
            + _page_specs(g_pages, LANE, 2 * C_DIM // LANE),
            out_specs=pl.BlockSpec((None, steps + 1, ss, w), lambda b, s, pt: (b, 0, 0, 0)),
            scratch_shapes=[pltpu.VMEM((C_IDX_HEADS * ss, LANE), F32), pltpu.VMEM((steps + 1, ss, w), jnp.int32)]),
        out_shape=jax.ShapeDtypeStruct((ns, steps + 1, ss, w), F32),
        compiler_params=_cparams("parallel", "arbitrary"), name="dsa_sample_index",
    )(page_table, iq, iw, c_rows, *([pool] * g_pages))
    m = C_HEADS * ss
    return pl.pallas_call(
        functools.partial(_dsa_sa_body, g_pages=g_pages, steps=steps, ss=ss),
        grid_spec=pltpu.PrefetchScalarGridSpec(
            num_scalar_prefetch=1, grid=(ns, steps),
            in_specs=[tok(C_HEADS * C_DIM), tok(row_w),
                      pl.BlockSpec((None, None, ss, w), lambda b, s, pt: (b, s, 0, 0)),
                      pl.BlockSpec((None, None, ss, w), lambda b, s, pt: (b, steps, 0, 0))]
            + _page_specs(g_pages, 2 * C_DIM, 0),
            out_specs=pl.BlockSpec((ss, C_HEADS * C_DIM), lambda b, s, pt: (b, 0)),
            scratch_shapes=[pltpu.VMEM((m, C_DIM), F32), pltpu.VMEM((m, 1), F32), pltpu.VMEM((m, 1), F32),
                            pltpu.VMEM((m, C_DIM), F32)]),
        out_shape=jax.ShapeDtypeStruct((ns * ss, C_HEADS * C_DIM), BF),
        compiler_params=_cparams("parallel", "arbitrary"), name="dsa_sample_attend",
    )(page_table, cq, c_rows, mask, mask, *([pool] * g_pages))


_CH = D_CMP_STRIDE
_SLC_PER_CH = D_SLC_BLOCK // D_CMP_STRIDE


def _round_up(n, m):
    return -(-n // m) * m


def _phi_weights(pe, w1, w2):
    nch = D_CMP_LEN // _CH
    halves = []
    for half in range(nch):
        wk = w1[0, half * _CH * D_DIM:(half + 1) * _CH * D_DIM].reshape(_CH, D_DIM, -1)
        wv = w1[1, half * _CH * D_DIM:(half + 1) * _CH * D_DIM].reshape(_CH, D_DIM, -1)
        z = jnp.zeros_like(wk)
        blk = jnp.concatenate([jnp.concatenate([wk, z], axis=2), jnp.concatenate([z, wv], axis=2)], axis=1)
        halves.append(blk.reshape(_CH * 2 * D_DIM, -1).astype(BF))
    pes = [jnp.concatenate([pe[0, half * _CH:(half + 1) * _CH], pe[1, half * _CH:(half + 1) * _CH]],
                           axis=1).reshape(1, -1) for half in range(nch)]
    z2 = jnp.zeros_like(w2[0])
    w2b = jnp.concatenate([jnp.concatenate([w2[0], z2], axis=1), jnp.concatenate([z2, w2[1]], axis=1)], axis=0)
    return halves[0], halves[1], pes[0], pes[1], w2b.astype(BF)


def _chunk_rows(read, n_chunks):
    return jnp.concatenate([read(j, n_chunks) for j in range(_CH)], axis=1)


def _compress_pre(x, wlo, whi, pelo, pehi):
    lo = _dot((x + pelo).astype(BF), wlo)
    hi = _dot((x + pehi).astype(BF), whi)
    return lo, hi


def _compress_post(lo, hi, w2b):
    n = lo.shape[0]
    return _dot(jax.nn.silu(lo + pltpu.roll(hi, n - 1, 0)).astype(BF), w2b)


def _cmp_branch(q, kvc, qpos, n_tok, nh, tq):
    ncp = kvc.shape[0]
    c_idx = lax.broadcasted_iota(jnp.int32, (tq, ncp), 1)
    vis = (c_idx < n_tok) & (c_idx * D_CMP_STRIDE + (D_CMP_LEN - 1) <= qpos)
    s = _add_head_mask(_dot_nt(q, kvc) * D_SCALE, jnp.where(vis, 0.0, NEG), nh)
    m = jnp.max(s, axis=-1, keepdims=True)
    p = jnp.where(s > 0.5 * NEG, jnp.exp(s - m), 0.0)
    l = jnp.sum(p, axis=-1, keepdims=True)
    l = jnp.where(l > 0.0, l, 1.0)
    o = _dot(p.astype(BF), kvc) / l
    pn = p / l
    psum = pn[0:tq]
    for h in range(1, nh):
        psum = psum + pn[h * tq:(h + 1) * tq]
    return o, psum


def _block_importance(psum, nsbp):
    ncp = psum.shape[1]
    c = lax.broadcasted_iota(jnp.int32, (ncp, nsbp), 0)
    j = lax.broadcasted_iota(jnp.int32, (ncp, nsbp), 1)
    rc = D_CMP_LEN // D_CMP_STRIDE
    band = jnp.where((c >= _SLC_PER_CH * j - (rc - 1)) & (c <= _SLC_PER_CH * j + _SLC_PER_CH - 1), 1.0, 0.0).astype(BF)
    hi = psum.astype(BF)
    r1 = psum - hi.astype(F32)
    mid = r1.astype(BF)
    lo = (r1 - mid.astype(F32)).astype(BF)
    return _dot(hi, band) + _dot(mid, band) + _dot(lo, band)


def _select_blocks(imp, qpos_col, nsb):
    jb = lax.broadcasted_iota(jnp.int32, imp.shape, 1)
    bt = qpos_col // D_SLC_BLOCK
    forced = (jb == 0) | (jb >= bt - (D_SLC_LOCAL - 1))
    val = jnp.where(forced, jnp.inf, imp)
    sel = _top_lanes(val, (jb <= bt) & (jb < nsb), min(D_SLC_TOPN, nsb))
    return jnp.where(sel, 1.0, 0.0)


def _slc_mask(selm, k0, tk, qpos):
    nsbp = selm.shape[1]
    jb = lax.broadcasted_iota(jnp.int32, (nsbp, tk), 0)
    kk = lax.broadcasted_iota(jnp.int32, (nsbp, tk), 1)
    expand = jnp.where((k0 + kk) // D_SLC_BLOCK == jb, 1.0, 0.0).astype(BF)
    chosen = _dot(selm.astype(BF), expand) > 0.5
    kpos = k0 + lax.broadcasted_iota(jnp.int32, (selm.shape[0], tk), 1)
    return jnp.where(chosen & (kpos <= qpos), 0.0, NEG)


def _nsa_combine(o_cmp, o_slc, o_win, gates, nh, tq, o_ref):
    lane = lax.broadcasted_iota(jnp.int32, (tq, LANE), 1)
    outs = []
    for h in range(nh):
        r = slice(h * tq, (h + 1) * tq)
        outs.append(gates[:, h:h + 1] * o_cmp[r] + gates[:, nh + h:nh + h + 1] * o_slc[r]
                    + gates[:, 2 * nh + h:2 * nh + h + 1] * o_win[r])
    for j in range(nh // 2):
        o_ref[:, j * LANE:(j + 1) * LANE] = jnp.where(lane < D_DIM, pltpu.roll(outs[2 * j], D_DIM, 1),
                                                      outs[2 * j + 1]).astype(o_ref.dtype)


def _nsa_p_body(dq_ref, gate_ref, cmp_ref, slc_ref, win_ref, wlo_ref, whi_ref, pelo_ref, pehi_ref, w2_ref, o_ref,
                qs, kvc, m_ref, l_ref, acc_ref, m2_ref, l2_ref, acc2_ref, *, tq, tk, sp):
    nh = D_HEADS
    qi = pl.program_id(1)
    q0 = qi * tq
    n_ch = sp // _CH
    ncp = kvc.shape[0]
    nsb = sp // D_SLC_BLOCK
    nsbp = _round_up(nsb, LANE)

    @pl.when(qi == 0)
    def _():
        x = _chunk_rows(lambda j, n: cmp_ref[pl.ds(j, n, stride=_CH), :], n_ch)
        lo, hi = _compress_pre(x, wlo_ref[...], whi_ref[...], pelo_ref[...], pehi_ref[...])
        kvc[...] = jnp.zeros(kvc.shape, kvc.dtype)
        kvc[0:n_ch, :] = _compress_post(lo, hi, w2_ref[...]).astype(kvc.dtype)

    for h in range(nh):
        qs[h * tq:(h + 1) * tq, :] = dq_ref[:, h * LANE:(h + 1) * LANE].astype(BF)
    q = qs[...]
    qpos_col = q0 + lax.broadcasted_iota(jnp.int32, (tq, 1), 0)
    o_cmp, psum = _cmp_branch(q, kvc[...], qpos_col, n_ch - 1, nh, tq)
    selm = _select_blocks(_block_importance(psum, nsbp), qpos_col, nsb)
    _softmax_init(m_ref, l_ref, acc_ref)
    for j in range(sp // tk):
        @pl.when(j * tk < q0 + tq)
        def _():
            kv = slc_ref[j * tk:(j + 1) * tk, :].astype(BF)
            maskf = _slc_mask(selm, j * tk, tk, qpos_col)
            _softmax_step(_add_head_mask(_dot_nt(q, kv) * D_SCALE, maskf, nh), kv, m_ref, l_ref, acc_ref)
    o_slc = _softmax_finish(l_ref, acc_ref)
    _softmax_init(m2_ref, l2_ref, acc2_ref)
    for w in range(D_WINDOW // tq + 1):
        k0 = q0 - D_WINDOW + w * tq

        @pl.when(k0 >= 0)
        def _():
            kv = win_ref[pl.ds(pl.multiple_of(k0, tq), tq), :].astype(BF)
            dist = qpos_col - (k0 + lax.broadcasted_iota(jnp.int32, (tq, tq), 1))
            maskf = jnp.where((dist >= 0) & (dist < D_WINDOW), 0.0, NEG)
            _softmax_step(_add_head_mask(_dot_nt(q, kv) * D_SCALE, maskf, nh), kv, m2_ref, l2_ref, acc2_ref)
    o_win = _softmax_finish(l2_ref, acc2_ref)
    _nsa_combine(o_cmp, o_slc, o_win, gate_ref[...], nh, tq, o_ref)


def nsa_prompt(dq, gates, d_rows, win_rows, phi, nb, sp):
    tq = _pick_tile(sp, 128, 8)
    tk = _pick_tile(sp, 256, LANE)
    assert D_WINDOW % tq == 0
    nq = sp // tq
    m = D_HEADS * tq
    ncp = _round_up(sp // _CH, LANE)
    tok = lambda w: pl.BlockSpec((tq, w), lambda b, i: (b * nq + i, 0))
    seq = lambda col: pl.BlockSpec((sp, LANE), lambda b, i: (b, col))
    stats = [pltpu.VMEM((m, 1), F32), pltpu.VMEM((m, 1), F32), pltpu.VMEM((m, LANE), F32)]
    return pl.pallas_call(
        functools.partial(_nsa_p_body, tq=tq, tk=tk, sp=sp), grid=(nb, nq),
        in_specs=[tok(D_HEADS * LANE), tok(LANE), seq(0), seq(1), seq(0)] + [_full_spec(a.shape) for a in phi],
        out_specs=tok(D_HEADS * D_DIM),
        out_shape=jax.ShapeDtypeStruct((nb * sp, D_HEADS * D_DIM), BF),
        scratch_shapes=[pltpu.VMEM((m, LANE), BF), pltpu.VMEM((ncp, LANE), BF)] + stats + stats,
        compiler_params=_cparams("parallel", "arbitrary"), name="nsa_prompt",
    )(dq, gates, d_rows, d_rows, win_rows, *phi)


def _nsa_sc_body(pt_ref, dq_ref, wlo_ref, whi_ref, pelo_ref, pehi_ref, w2_ref, *rest,
                 g_pages, steps, ss, past_len, nsb):
    pages = rest[:g_pages]
    ocmp_ref, selm_ref, qs, los, his = rest[g_pages:]
    nh = D_HEADS
    step = pl.program_id(1)
    cpp = PAGE_SIZE // _CH
    n_ch = past_len // _CH

    @pl.when(step == 0)
    def _():
        for h in range(nh):
            qs[h * ss:(h + 1) * ss, :] = dq_ref[:, h * LANE:(h + 1) * LANE]

    x = jnp.concatenate([_chunk_rows(lambda j, n: p[pl.ds(j, n, stride=_CH), :], cpp) for p in pages], axis=0)
    lo, hi = _compress_pre(x, wlo_ref[...], whi_ref[...], pelo_ref[...], pehi_ref[...])
    r0 = pl.multiple_of(step * (g_pages * cpp), g_pages * cpp)
    los[pl.ds(r0, g_pages * cpp), :] = lo
    his[pl.ds(r0, g_pages * cpp), :] = hi

    @pl.when(step == steps - 1)
    def _():
        kvc = _compress_post(los[...], his[...], w2_ref[...]).astype(BF)
        qpos_col = past_len + lax.broadcasted_iota(jnp.int32, (ss, 1), 0)
        o_cmp, psum = _cmp_branch(qs[...].astype(BF), kvc, qpos_col, n_ch - 1, nh, ss)
        ocmp_ref[...] = o_cmp
        selm_ref[...] = _select_blocks(_block_importance(psum, selm_ref.shape[-1]), qpos_col, nsb)


def _nsa_sa_body(pt_ref, dq_ref, gate_ref, new_ref, wnew_ref, wstate_ref, ocmp_ref, selm_ref, *rest,
                 g_pages, steps, ss, past_len):
    pages = rest[:g_pages]
    o_ref, qs, m_ref, l_ref, acc_ref = rest[g_pages:]
    nh = D_HEADS
    step = pl.program_id(1)
    w = g_pages * PAGE_SIZE

    @pl.when(step == 0)
    def _():
        for h in range(nh):
            qs[h * ss:(h + 1) * ss, :] = dq_ref[:, h * LANE:(h + 1) * LANE]
        _softmax_init(m_ref, l_ref, acc_ref)

    q = qs[...].astype(BF)
    qpos_col = past_len + lax.broadcasted_iota(jnp.int32, (ss, 1), 0)
    selm = selm_ref[...]
    kv = jnp.concatenate([p[...] for p in pages], axis=0).astype(BF)
    maskf = _slc_mask(selm, step * w, w, qpos_col)
    _softmax_step(_add_head_mask(_dot_nt(q, kv) * D_SCALE, maskf, nh), kv, m_ref, l_ref, acc_ref)

    @pl.when(step == steps - 1)
    def _():
        kvn = new_ref[:, LANE:2 * LANE].astype(BF)
        maskn = _slc_mask(selm, past_len, ss, qpos_col)
        _softmax_step(_add_head_mask(_dot_nt(q, kvn) * D_SCALE, maskn, nh), kvn, m_ref, l_ref, acc_ref)
        o_slc = _softmax_finish(l_ref, acc_ref)
        _softmax_init(m_ref, l_ref, acc_ref)
        wbuf = wstate_ref.shape[0]
        for kvw, k0 in ((wstate_ref[...].astype(BF), past_len - wbuf), (wnew_ref[...].astype(BF), past_len)):
            n = kvw.shape[0]
            dist = qpos_col - (k0 + lax.broadcasted_iota(jnp.int32, (ss, n), 1))
            maskw = jnp.where((dist >= 0) & (dist < D_WINDOW), 0.0, NEG)
            _softmax_step(_add_head_mask(_dot_nt(q, kvw) * D_SCALE, maskw, nh), kvw, m_ref, l_ref, acc_ref)
        o_win = _softmax_finish(l_ref, acc_ref)
        _nsa_combine(ocmp_ref[...], o_slc, o_win, gate_ref[...], nh, ss, o_ref)


def nsa_sample(dq, gates, d_rows, win_rows, win_state, phi, pool, page_table, tp, ns, ss):
    n_pages = page_table.shape[1]
    g_pages = _pages_per_step(n_pages)
    steps = n_pages // g_pages
    past_len = n_pages * PAGE_SIZE
    assert ss <= D_CMP_STRIDE and past_len % D_SLC_BLOCK == 0
    n_ch = past_len // _CH
    nsb = -(-(past_len + ss) // D_SLC_BLOCK)
    nsbp = _round_up(nsb, LANE)
    m = D_HEADS * ss
    tok = lambda wd: pl.BlockSpec((ss, wd), lambda b, s, pt: (tp // ss + b, 0))
    per_seq = lambda shape: pl.BlockSpec((None,) + shape, lambda b, s, pt: (b,) + (0,) * len(shape))
    phi_specs = [pl.BlockSpec(a.shape, lambda b, s, pt, nd=a.ndim: (0,) * nd) for a in phi]
    o_cmp, selm = pl.pallas_call(
        functools.partial(_nsa_sc_body, g_pages=g_pages, steps=steps, ss=ss, past_len=past_len, nsb=nsb),
        grid_spec=pltpu.PrefetchScalarGridSpec(
            num_scalar_prefetch=1, grid=(ns, steps),
            in_specs=[tok(D_HEADS * LANE)] + phi_specs + _page_specs(g_pages, LANE, 0),
            out_specs=[per_seq((m, LANE)), per_seq((ss, nsbp))],
            scratch_shapes=[pltpu.VMEM((m, LANE), F32), pltpu.VMEM((n_ch, LANE), F32), pltpu.VMEM((n_ch, LANE), F32)]),
        out_shape=[jax.ShapeDtypeStruct((ns, m, LANE), F32), jax.ShapeDtypeStruct((ns, ss, nsbp), F32)],
        compiler_params=_cparams("parallel", "arbitrary"), name="nsa_sample_compress",
    )(page_table, dq, *phi, *([pool] * g_pages))
    wbuf = win_state.shape[1]
    return pl.pallas_call(
        functools.partial(_nsa_sa_body, g_pages=g_pages, steps=steps, ss=ss, past_len=past_len),
        grid_spec=pltpu.PrefetchScalarGridSpec(
            num_scalar_prefetch=1, grid=(ns, steps),
            in_specs=[tok(D_HEADS * LANE), tok(LANE), tok(4 * D_DIM), tok(LANE), per_seq((wbuf, LANE)),
                      per_seq((m, LANE)), per_seq((ss, nsbp))] + _page_specs(g_pages, LANE, 1),
            out_specs=pl.BlockSpec((ss, D_HEADS * D_DIM), lambda b, s, pt: (b, 0)),
            scratch_shapes=[pltpu.VMEM((m, LANE), F32), pltpu.VMEM((m, 1), F32), pltpu.VMEM((m, 1), F32),
                            pltpu.VMEM((m, LANE), F32)]),
        out_shape=jax.ShapeDtypeStruct((ns * ss, D_HEADS * D_DIM), BF),
        compiler_params=_cparams("parallel", "arbitrary"), name="nsa_sample_attend",
    )(page_table, dq, gates, d_rows, win_rows, win_state, o_cmp, selm, *([pool] * g_pages))


def odd_mixers(h, pos, tp, nb, sp, ns, ss, pool_c, pool_d, page_table, win_state, phi_pe, phi_w1, phi_w2):
    tab_a = _rope_tables(pos, C_DIM // 4, ROPE_THETA, C_DIM)
    tab_i = _rope_tables(pos, C_IDX_DIM // 4, ROPE_THETA, C_IDX_DIM)
    cq, c_rows, iq, iw, dq, d_rows, win_rows, gates = prep_odd(h, tab_a, tab_i)
    phi = _phi_weights(phi_pe, phi_w1, phi_w2)
    o_c = jnp.concatenate([dsa_prompt(cq, iq, iw, c_rows, nb, sp),
                           dsa_sample(cq, iq, iw, c_rows, pool_c, page_table, tp, ns, ss)], axis=0)
    o_d = jnp.concatenate([nsa_prompt(dq, gates, d_rows, win_rows, phi, nb, sp),
                           nsa_sample(dq, gates, d_rows, win_rows, win_state, phi, pool_d, page_table, tp, ns, ss)],
                          axis=0)
    return jnp.concatenate([o_c, o_d], axis=1), c_rows, d_rows, win_rows


def _pad_cols(w, mult):
    n = w.shape[1]
    npad = -(-n // mult) * mult
    return jnp.pad(w, ((0, 0), (0, npad - n))) if npad != n else w


def kernel(x_prompt, x_sample, cache_a_kv, cache_b_latent, cache_c_kvi, cache_d_kv, state_d_win, page_table,
           w_in_even, b_g_cq, b_g_ckv, b_w_uq, b_w_ukv, w_out_even,
           w_in_odd, d_phi_pe, d_phi_w1, d_phi_w2, w_out_odd,
           mlp_w1, mlp_w2, ln_g, ln_b):
    nb, sp, d = x_prompt.shape
    ns, ss, _ = x_sample.shape
    tp = nb * sp
    past_len = page_table.shape[1] * PAGE_SIZE
    pos_p = jnp.arange(sp, dtype=jnp.int32)
    pos_s = past_len + jnp.arange(ss, dtype=jnp.int32)
    bf = jnp.bfloat16
    x = jnp.concatenate([x_prompt.reshape(tp, d), x_sample.reshape(ns * ss, d)], axis=0)

    def groups(h):
        return h[:tp].reshape(nb, sp, -1), h[tp:].reshape(ns, ss, -1)

    def post(x, o, w_out, layer):
        x = resid_layer_norm(x, matmul(o, w_out.astype(bf)), ln_g[layer, 0], ln_b[layer, 0])
        hm = matmul(x, mlp_w1[layer].astype(bf), act="relu2", out_dtype=bf)
        return resid_layer_norm(x, matmul(hm, mlp_w2[layer].astype(bf)), ln_g[layer, 1], ln_b[layer, 1])

    pos_all = jnp.concatenate([jnp.tile(pos_p, nb), jnp.tile(pos_s, ns)])
    h0 = matmul(x, _pad_cols(w_in_even, LANE).astype(bf))
    o0, a_rows, b_rows = even_mixers(h0, pos_all, tp, nb, sp, ns, ss, cache_a_kv, cache_b_latent, page_table,
                                     b_g_cq, b_g_ckv, b_w_uq, b_w_ukv)
    a_p, a_s = groups(a_rows)
    b_p, b_s = groups(b_rows)
    x = post(x, o0, w_out_even, 0)
    h1 = matmul(x, _odd_weight_cols(w_in_odd).astype(bf))
    o1, c_rows, d_rows, win_rows = odd_mixers(h1, pos_all, tp, nb, sp, ns, ss, cache_c_kvi, cache_d_kv, page_table,
                                              state_d_win, d_phi_pe, d_phi_w1, d_phi_w2)
    c_p, c_s = groups(c_rows)
    d_p, d_s = groups(d_rows)
    win_p, win_s = groups(win_rows)
    w_p = win_p[:, -min(D_WINDOW, sp):]
    w_s = jnp.concatenate([state_d_win, win_s], axis=1)[:, -state_d_win.shape[1]:]
    x = post(x, o1, w_out_odd, 1)
    y_p, y_s = groups(x)
    return (y_p, y_s, a_p, a_s, b_p, b_s, c_p, c_s, d_p, d_s, w_p, w_s)
```

```python
import functools

import jax
import jax.numpy as jnp
import numpy as np
from jax import lax
from jax.experimental import pallas as pl
from jax.experimental.pallas import tpu as pltpu

DEPTH = 2
PAGE_SIZE = 128
ROPE_THETA = 500000.0
LN_EPS = 1e-5
RMS_EPS = 1e-6
DN_ALPHA = (2 * DEPTH) ** 0.25

A_DIM = 128
A_HEADS = 16
A_BLOCK = 256
A_ROW = 2 * A_DIM
A_TOPK = 3
A_SCALE = A_DIM ** -0.5

B_HEADS = 16
B_NOPE = 128
B_ROPE = 64
B_VDIM = 128
B_QRANK = 768
B_KVRANK = 256
B_ROPE_THETA = 10000.0
B_SCALE = (B_NOPE + B_ROPE) ** -0.5

C_DIM = 128
C_HEADS = 16
C_IDX_HEADS = 32
C_IDX_DIM = 64
C_TOPK = 256
C_SCALE = C_DIM ** -0.5
C_IDX_W_SCALE = (C_IDX_HEADS * C_IDX_DIM) ** -0.5

D_DIM = 64
D_HEADS = 32
D_CMP_LEN = 32
D_CMP_STRIDE = 16
D_SLC_BLOCK = 64
D_SLC_TOPN = 16
D_SLC_LOCAL = 2
D_WINDOW = 512
D_SCALE = D_DIM ** -0.5

EVEN_SPLITS = (A_HEADS * A_DIM, A_DIM, A_DIM, B_QRANK, B_KVRANK, B_ROPE)
ODD_SPLITS = (C_HEADS * C_DIM, C_DIM, C_DIM, C_IDX_HEADS * C_IDX_DIM, C_IDX_DIM, C_IDX_HEADS,
              D_HEADS * D_DIM, 6 * D_DIM, 3 * D_HEADS)

LANE = 128
VMEM_LIMIT_BYTES = 56 * 1024 * 1024


def _cparams(*sem):
    return pltpu.CompilerParams(dimension_semantics=sem, vmem_limit_bytes=VMEM_LIMIT_BYTES)


def _pick_tile(n, cap, mult):
    best = None
    for t in range(mult, min(n, cap) + 1, mult):
        if n % t == 0:
            best = t
    assert best is not None, (n, cap, mult)
    return best


def _mm_body(x_ref, w_ref, o_ref, acc_ref, *, nk, act):
    k = pl.program_id(2)

    @pl.when(k == 0)
    def _():
        acc_ref[...] = jnp.zeros_like(acc_ref)

    acc_ref[...] += jnp.dot(x_ref[...].astype(jnp.bfloat16), w_ref[...],
                            preferred_element_type=jnp.float32)

    @pl.when(k == nk - 1)
    def _():
        r = acc_ref[...]
        if act == "relu2":
            r = jnp.maximum(r, 0.0)
            r = r * r
        o_ref[...] = r.astype(o_ref.dtype)


def matmul(x, w, *, act=None, out_dtype=jnp.float32):
    m, kdim = x.shape
    n = w.shape[1]
    tm = _pick_tile(m, 1024, 8)
    tn = _pick_tile(n, 2048, LANE)
    tk = _pick_tile(kdim, 512, LANE)
    nk = kdim // tk
    return pl.pallas_call(
        functools.partial(_mm_body, nk=nk, act=act),
        grid=(m // tm, n // tn, nk),
        in_specs=[pl.BlockSpec((tm, tk), lambda i, j, k: (i, k)),
                  pl.BlockSpec((tk, tn), lambda i, j, k: (k, j))],
        out_specs=pl.BlockSpec((tm, tn), lambda i, j, k: (i, j)),
        out_shape=jax.ShapeDtypeStruct((m, n), out_dtype),
        scratch_shapes=[pltpu.VMEM((tm, tn), jnp.float32)],
        compiler_params=_cparams("parallel", "parallel", "arbitrary"),
        name="matmul",
    )(x, w)


def _ln_body(x_ref, o_ref, g_ref, b_ref, y_ref, yb_ref):
    y = DN_ALPHA * x_ref[...] + o_ref[...]
    mu = jnp.mean(y, axis=-1, keepdims=True)
    yc = y - mu
    var = jnp.mean(yc * yc, axis=-1, keepdims=True)
    out = yc * lax.rsqrt(var + LN_EPS) * g_ref[...] + b_ref[...]
    y_ref[...] = out
    yb_ref[...] = out.astype(yb_ref.dtype)


def resid_layer_norm(x, o, g, b):
    m, d = x.shape
    tm = _pick_tile(m, 256, 8)
    row = pl.BlockSpec((tm, d), lambda i: (i, 0))
    vec = pl.BlockSpec((1, d), lambda i: (0, 0))
    return pl.pallas_call(
        _ln_body, grid=(m // tm,), in_specs=[row, row, vec, vec], out_specs=[row, row],
        out_shape=[jax.ShapeDtypeStruct((m, d), jnp.float32), jax.ShapeDtypeStruct((m, d), jnp.bfloat16)],
        compiler_params=_cparams("parallel"), name="resid_layer_norm",
    )(x, o, g.reshape(1, d), b.reshape(1, d))


NEG = -1e30
M_FLOOR = -1e29
BF = jnp.bfloat16
F32 = jnp.float32


def _dot(a, b):
    return jnp.dot(a, b, preferred_element_type=F32)


def _dot_nt(a, b):
    return lax.dot_general(a, b, (((1,), (1,)), ((), ())), preferred_element_type=F32)


def _rope(x, c, sm, sp, half):
    return x * c + pltpu.roll(x, LANE - half, 1) * sm + pltpu.roll(x, half, 1) * sp


def _add_head_mask(s, maskf, nh):
    tq, tk = maskf.shape
    return (s.reshape(nh, tq, tk) + maskf[None]).reshape(nh * tq, tk)


def _softmax_init(m_ref, l_ref, acc_ref):
    m_ref[...] = jnp.full(m_ref.shape, M_FLOOR, F32)
    l_ref[...] = jnp.zeros(l_ref.shape, F32)
    acc_ref[...] = jnp.zeros(acc_ref.shape, F32)


def _lane_partial_sum(p):
    tk = p.shape[1]
    if tk % LANE == 0:
        part = p[:, 0:LANE]
        for c in range(1, tk // LANE):
            part = part + p[:, c * LANE:(c + 1) * LANE]
        return part
    lane = lax.broadcasted_iota(jnp.int32, (p.shape[0], LANE), 1)
    return jnp.where(lane == 0, jnp.sum(p, axis=-1, keepdims=True), 0.0)


def _softmax_step(s, v, m_ref, l_ref, acc_ref, v_is_transposed=False):
    m_prev = m_ref[...]
    m_new = jnp.maximum(m_prev, jnp.max(s, axis=-1, keepdims=True))
    alpha = jnp.exp(m_prev - m_new)
    p = jnp.exp(s - m_new)
    l_ref[...] = alpha * l_ref[...] + _lane_partial_sum(p)
    pv = _dot_nt(p.astype(BF), v) if v_is_transposed else _dot(p.astype(BF), v)
    acc_ref[...] = alpha * acc_ref[...] + pv
    m_ref[...] = m_new


def _softmax_finish(l_ref, acc_ref):
    l = jnp.sum(l_ref[...], axis=-1, keepdims=True)
    return acc_ref[...] / jnp.where(l > 0.0, l, 1.0)


def _top_lanes(g, avail, k):
    lane = lax.broadcasted_iota(jnp.int32, g.shape, 1)
    sel = jnp.zeros(g.shape, jnp.bool_)
    for _ in range(k):
        gm = jnp.where(avail, g, -jnp.inf)
        mx = jnp.max(gm, axis=-1, keepdims=True)
        idx = jnp.min(jnp.where(avail & (gm == mx), lane, jnp.int32(2 ** 30)), axis=-1, keepdims=True)
        pick = lane == idx
        sel = sel | pick
        avail = avail & jnp.logical_not(pick)
    return sel


def _rope_tables(pos, rot_dim, theta, width):
    half = rot_dim // 2
    inv = theta ** (-jnp.arange(0, rot_dim, 2, dtype=jnp.float32) / rot_dim)
    ang = pos.astype(jnp.float32)[:, None] * inv[None, :]
    c, s = jnp.cos(ang), jnp.sin(ang)
    t = pos.shape[0]
    one = jnp.ones((t, width - rot_dim), F32)
    z_half = jnp.zeros((t, half), F32)
    z_rest = jnp.zeros((t, width - rot_dim), F32)
    rep = LANE // width
    cc = jnp.tile(jnp.concatenate([c, c, one], axis=1), (1, rep))
    sm = jnp.tile(jnp.concatenate([-s, z_half, z_rest], axis=1), (1, rep))
    sp = jnp.tile(jnp.concatenate([z_half, s, z_rest], axis=1), (1, rep))
    return cc, sm, sp


def _tok_spec(tm, w):
    return pl.BlockSpec((tm, w), lambda i: (i, 0))


def _full_spec(shape):
    nd = len(shape)
    return pl.BlockSpec(shape, lambda *_: (0,) * nd)


_E_AK, _E_AV, _E_CQ, _E_CKV, _E_KR = 2048, 2176, 2304, 3072, 3328


def _prep_even_body(h_ref, ca, sma, spa, cb, smb, spb, gq_ref, gkv_ref,
                    aq_ref, arows_ref, cqn_ref, brows_ref):
    c, sm, sp = ca[...], sma[...], spa[...]
    ha = A_DIM // 8
    for h in range(A_HEADS):
        aq_ref[:, h * LANE:(h + 1) * LANE] = _rope(h_ref[:, h * LANE:(h + 1) * LANE], c, sm, sp, ha)
    arows_ref[:, 0:A_DIM] = _rope(h_ref[:, _E_AK:_E_AK + A_DIM], c, sm, sp, ha)
    arows_ref[:, A_DIM:2 * A_DIM] = h_ref[:, _E_AV:_E_AV + A_DIM]
    cq = h_ref[:, _E_CQ:_E_CQ + B_QRANK]
    cqn = cq * lax.rsqrt(jnp.mean(cq * cq, axis=-1, keepdims=True) + RMS_EPS) * gq_ref[...]
    cqn_ref[...] = cqn.astype(cqn_ref.dtype)
    ckv = h_ref[:, _E_CKV:_E_CKV + B_KVRANK]
    brows_ref[:, 0:B_KVRANK] = ckv * lax.rsqrt(jnp.mean(ckv * ckv, axis=-1, keepdims=True) + RMS_EPS) * gkv_ref[...]
    kr = _rope(h_ref[:, _E_KR:_E_KR + LANE], cb[...], smb[...], spb[...], B_ROPE // 2)
    brows_ref[:, B_KVRANK:B_KVRANK + B_ROPE] = kr[:, 0:B_ROPE]


def prep_even(h, tab_a, tab_b, g_cq, g_ckv):
    t, w = h.shape
    tm = _pick_tile(t, 256, 8)
    tab = _tok_spec(tm, LANE)
    return pl.pallas_call(
        _prep_even_body, grid=(t // tm,),
        in_specs=[_tok_spec(tm, w), tab, tab, tab, tab, tab, tab,
                  _full_spec((1, B_QRANK)), _full_spec((1, B_KVRANK))],
        out_specs=[_tok_spec(tm, A_HEADS * A_DIM), _tok_spec(tm, A_ROW),
                   _tok_spec(tm, B_QRANK), _tok_spec(tm, B_KVRANK + B_ROPE)],
        out_shape=[jax.ShapeDtypeStruct((t, A_HEADS * A_DIM), F32),
                   jax.ShapeDtypeStruct((t, A_ROW), F32),
                   jax.ShapeDtypeStruct((t, B_QRANK), BF),
                   jax.ShapeDtypeStruct((t, B_KVRANK + B_ROPE), F32)],
        compiler_params=_cparams("parallel"), name="prep_even",
    )(h, *tab_a, *tab_b, g_cq.reshape(1, -1), g_ckv.reshape(1, -1))


def _mla_qprep_body(q_ref, wuk_ref, cb, smb, spb, ql_ref, qr_ref):
    for h in range(B_HEADS):
        qn = q_ref[:, h * B_NOPE:(h + 1) * B_NOPE].astype(BF)
        ql_ref[:, h * B_KVRANK:(h + 1) * B_KVRANK] = _dot(qn, wuk_ref[h])
    c, sm, sp = cb[...], smb[...], spb[...]
    lane = lax.broadcasted_iota(jnp.int32, c.shape, 1)
    base = B_HEADS * B_NOPE
    for j in range(B_HEADS // 2):
        r = _rope(q_ref[:, base + j * LANE:base + (j + 1) * LANE], c, sm, sp, B_ROPE // 2)
        qr_ref[:, (2 * j) * LANE:(2 * j + 1) * LANE] = jnp.where(lane < B_ROPE, r, 0.0)
        qr_ref[:, (2 * j + 1) * LANE:(2 * j + 2) * LANE] = jnp.where(lane < B_ROPE, pltpu.roll(r, B_ROPE, 1), 0.0)


def mla_qprep(q, wuk_t, tab_b):
    t, w = q.shape
    tm = _pick_tile(t, 256, 8)
    tab = _tok_spec(tm, LANE)
    return pl.pallas_call(
        _mla_qprep_body, grid=(t // tm,),
        in_specs=[_tok_spec(tm, w), _full_spec(wuk_t.shape), tab, tab, tab],
        out_specs=[_tok_spec(tm, B_HEADS * B_KVRANK), _tok_spec(tm, B_HEADS * LANE)],
        out_shape=[jax.ShapeDtypeStruct((t, B_HEADS * B_KVRANK), F32),
                   jax.ShapeDtypeStruct((t, B_HEADS * LANE), F32)],
        compiler_params=_cparams("parallel"), name="mla_qprep",
    )(q, wuk_t, *tab_b)


def _mla_oproj_body(ol_ref, wuv_ref, o_ref):
    for h in range(B_HEADS):
        ol = ol_ref[:, h * B_KVRANK:(h + 1) * B_KVRANK].astype(BF)
        o_ref[:, h * B_VDIM:(h + 1) * B_VDIM] = _dot(ol, wuv_ref[h]).astype(o_ref.dtype)


def mla_oproj(o_lat, wuv):
    t, w = o_lat.shape
    tm = _pick_tile(t, 256, 8)
    return pl.pallas_call(
        _mla_oproj_body, grid=(t // tm,),
        in_specs=[_tok_spec(tm, w), _full_spec(wuv.shape)],
        out_specs=_tok_spec(tm, B_HEADS * B_VDIM),
        out_shape=jax.ShapeDtypeStruct((t, B_HEADS * B_VDIM), BF),
        compiler_params=_cparams("parallel"), name="mla_oproj",
    )(o_lat, wuv)


def _pages_per_step(n_pages, cap=16):
    return _pick_tile(n_pages, cap, 1)


def _page_specs(g_pages, width, col_block):
    def spec(g):
        return pl.BlockSpec((None, PAGE_SIZE, width),
                            lambda b, s, pt: (pt[b, s * g_pages + g], 0, col_block))
    return [spec(g) for g in range(g_pages)]


def _cat_pages(pages, lo, hi):
    return jnp.concatenate([p[:, lo:hi] for p in pages], axis=0)


def _page_specs_t(g_pages, feats, feat_block):
    def spec(g):
        return pl.BlockSpec((None, feats, PAGE_SIZE),
                            lambda b, s, pt: (pt[b, s * g_pages + g], feat_block, 0))
    return [spec(g) for g in range(g_pages)]


def _cat_pages_t(pages, lo, hi):
    return jnp.concatenate([p[lo:hi, :] for p in pages], axis=1)


def _moba_p_body(aq_ref, rows_ref, o_ref, qs, m_ref, l_ref, acc_ref, *, tq, sp):
    nh = A_HEADS
    qi = pl.program_id(1)
    q0 = qi * tq
    own = q0 // A_BLOCK
    nblk = sp // A_BLOCK
    for h in range(nh):
        qs[h * tq:(h + 1) * tq, :] = aq_ref[:, h * A_DIM:(h + 1) * A_DIM].astype(BF)
    q = qs[...]
    km = [jnp.mean(rows_ref[n * A_BLOCK:(n + 1) * A_BLOCK, 0:A_DIM], axis=0, keepdims=True) for n in range(nblk)]
    km = jnp.concatenate(km + [jnp.zeros((LANE - nblk, A_DIM), F32)], axis=0)
    gate = _dot_nt(q, km.astype(BF))
    lane = lax.broadcasted_iota(jnp.int32, gate.shape, 1)
    selm = jnp.where(_top_lanes(gate, lane < own, A_TOPK), 0.0, NEG)
    _softmax_init(m_ref, l_ref, acc_ref)
    for n in range(nblk):
        @pl.when(n <= own)
        def _():
            k = rows_ref[n * A_BLOCK:(n + 1) * A_BLOCK, 0:A_DIM].astype(BF)
            v = rows_ref[n * A_BLOCK:(n + 1) * A_BLOCK, A_DIM:2 * A_DIM].astype(BF)
            s = _dot_nt(q, k) * A_SCALE
            qpos = q0 + lax.broadcasted_iota(jnp.int32, (tq, A_BLOCK), 0)
            kpos = n * A_BLOCK + lax.broadcasted_iota(jnp.int32, (tq, A_BLOCK), 1)
            causal = jnp.where(kpos <= qpos, 0.0, NEG)
            s_own = _add_head_mask(s, causal, nh)
            s_past = s + selm[:, n:n + 1]
            _softmax_step(jnp.where(n == own, s_own, s_past), v, m_ref, l_ref, acc_ref)
    res = _softmax_finish(l_ref, acc_ref)
    for h in range(nh):
        o_ref[:, h * A_DIM:(h + 1) * A_DIM] = res[h * tq:(h + 1) * tq, :].astype(o_ref.dtype)


def moba_prompt(aq, a_rows, nb, sp):
    tq = _pick_tile(sp, 128, 8)
    nq = sp // tq
    m = A_HEADS * tq
    return pl.pallas_call(
        functools.partial(_moba_p_body, tq=tq, sp=sp), grid=(nb, nq),
        in_specs=[pl.BlockSpec((tq, A_HEADS * A_DIM), lambda b, i: (b * nq + i, 0)),
                  pl.BlockSpec((sp, A_ROW), lambda b, i: (b, 0))],
        out_specs=pl.BlockSpec((tq, A_HEADS * A_DIM), lambda b, i: (b * nq + i, 0)),
        out_shape=jax.ShapeDtypeStruct((nb * sp, A_HEADS * A_DIM), BF),
        scratch_shapes=[pltpu.VMEM((m, A_DIM), BF), pltpu.VMEM((m, 1), F32), pltpu.VMEM((m, LANE), F32),
                        pltpu.VMEM((m, A_DIM), F32)],
        compiler_params=_cparams("parallel", "arbitrary"), name="moba_prompt",
    )(aq, a_rows)


def _moba_s_body(pt_ref, aq_ref, new_ref, *rest, g_pages, steps, ss):
    pages = rest[:g_pages]
    o_ref, qs, gs, ms, ls, accs = rest[g_pages:]
    nh = A_HEADS
    m_rows = nh * ss
    step = pl.program_id(1)
    bps = g_pages * PAGE_SIZE // A_BLOCK
    ppb = A_BLOCK // PAGE_SIZE

    @pl.when(step == 0)
    def _():
        for h in range(nh):
            qs[h * ss:(h + 1) * ss, :] = aq_ref[:, h * A_DIM:(h + 1) * A_DIM]
        gs[...] = jnp.full(gs.shape, NEG, F32)
        ms[...] = jnp.full(ms.shape, NEG, F32)
        ls[...] = jnp.zeros(ls.shape, F32)

    q = qs[...].astype(BF)
    lane = lax.broadcasted_iota(jnp.int32, (m_rows, LANE), 1)
    for i in range(bps):
        n = step * bps + i
        blk = pages[i * ppb:(i + 1) * ppb]
        k = _cat_pages(blk, 0, A_DIM).astype(BF)
        v = _cat_pages(blk, A_DIM, 2 * A_DIM).astype(BF)
        s = _dot_nt(q, k) * A_SCALE
        m_n = jnp.max(s, axis=-1, keepdims=True)
        p = jnp.exp(s - m_n)
        here = lane == n
        gs[...] = jnp.where(here, jnp.mean(s, axis=-1, keepdims=True), gs[...])
        ms[...] = jnp.where(here, m_n, ms[...])
        ls[...] = jnp.where(here, jnp.sum(p, axis=-1, keepdims=True), ls[...])
        accs[n] = _dot(p.astype(BF), v)

    @pl.when(step == steps - 1)
    def _():
        nblk = steps * bps
        sel = _top_lanes(gs[...], lane < nblk, A_TOPK)
        kn = new_ref[:, 0:A_DIM].astype(BF)
        vn = new_ref[:, A_DIM:2 * A_DIM].astype(BF)
        qrow = lax.broadcasted_iota(jnp.int32, (ss, ss), 0)
        kcol = lax.broadcasted_iota(jnp.int32, (ss, ss), 1)
        s_own = _add_head_mask(_dot_nt(q, kn) * A_SCALE, jnp.where(kcol <= qrow, 0.0, NEG), nh)
        m_own = jnp.max(s_own, axis=-1, keepdims=True)
        p_own = jnp.where(s_own > 0.5 * NEG, jnp.exp(s_own - m_own), 0.0)
        m_fin = jnp.maximum(jnp.max(jnp.where(sel, ms[...], NEG), axis=-1, keepdims=True), m_own)
        w = jnp.where(sel, jnp.exp(ms[...] - m_fin), 0.0)
        w_own = jnp.exp(m_own - m_fin)
        l_fin = jnp.sum(w * ls[...], axis=-1, keepdims=True) + w_own * jnp.sum(p_own, axis=-1, keepdims=True)
        acc = w_own * _dot(p_own.astype(BF), vn)
        for n in range(nblk):
            acc = acc + w[:, n:n + 1] * accs[n]
        res = acc / l_fin
        for h in range(nh):
            o_ref[:, h * A_DIM:(h + 1) * A_DIM] = res[h * ss:(h + 1) * ss, :].astype(o_ref.dtype)


def moba_sample(aq, a_rows, pool, page_table, tp, ns, ss):
    n_pages = page_table.shape[1]
    g_pages = _pages_per_step(n_pages)
    steps = n_pages // g_pages
    assert g_pages % (A_BLOCK // PAGE_SIZE) == 0 and n_pages * PAGE_SIZE // A_BLOCK <= LANE
    m = A_HEADS * ss
    tok = lambda w: pl.BlockSpec((ss, w), lambda b, s, pt: (tp // ss + b, 0))
    grid_spec = pltpu.PrefetchScalarGridSpec(
        num_scalar_prefetch=1, grid=(ns, steps),
        in_specs=[tok(A_HEADS * A_DIM), tok(A_ROW)] + _page_specs(g_pages, A_ROW, 0),
        out_specs=pl.BlockSpec((ss, A_HEADS * A_DIM), lambda b, s, pt: (b, 0)),
        scratch_shapes=[pltpu.VMEM((m, A_DIM), F32), pltpu.VMEM((m, LANE), F32), pltpu.VMEM((m, LANE), F32),
                        pltpu.VMEM((m, LANE), F32),
                        pltpu.VMEM((n_pages * PAGE_SIZE // A_BLOCK, m, A_DIM), F32)])
    return pl.pallas_call(
        functools.partial(_moba_s_body, g_pages=g_pages, steps=steps, ss=ss), grid_spec=grid_spec,
        out_shape=jax.ShapeDtypeStruct((ns * ss, A_HEADS * A_DIM), BF),
        compiler_params=_cparams("parallel", "arbitrary"), name="moba_sample",
    )(page_table, aq, a_rows, *([pool] * g_pages))


def _mla_scores(qls, qrs, blk):
    ckv = blk[:, 0:B_KVRANK].astype(BF)
    kr = blk[:, B_KVRANK:B_KVRANK + B_ROPE].astype(BF)
    return (_dot_nt(qls, ckv) + _dot_nt(qrs, kr)) * B_SCALE, ckv


def _mla_p_body(ql_ref, qr_ref, rows_ref, o_ref, qls, qrs, m_ref, l_ref, acc_ref, *, tq, tk):
    nh = B_HEADS
    qi = pl.program_id(1)
    q0 = qi * tq
    for h in range(nh):
        qls[h * tq:(h + 1) * tq, :] = ql_ref[:, h * B_KVRANK:(h + 1) * B_KVRANK].astype(BF)
        qrs[h * tq:(h + 1) * tq, :] = qr_ref[:, h * LANE:h * LANE + B_ROPE].astype(BF)
    _softmax_init(m_ref, l_ref, acc_ref)

    def body(j, carry):
        k0 = pl.multiple_of(j * tk, tk)
        s, ckv = _mla_scores(qls[...], qrs[...], rows_ref[pl.ds(k0, tk), :])
        qpos = q0 + lax.broadcasted_iota(jnp.int32, (tq, tk), 0)
        kpos = k0 + lax.broadcasted_iota(jnp.int32, (tq, tk), 1)
        s = _add_head_mask(s, jnp.where(kpos <= qpos, 0.0, NEG), nh)
        _softmax_step(s, ckv, m_ref, l_ref, acc_ref)
        return carry

    lax.fori_loop(0, (q0 + tq + tk - 1) // tk, body, 0)
    res = _softmax_finish(l_ref, acc_ref)
    for h in range(nh):
        o_ref[:, h * B_KVRANK:(h + 1) * B_KVRANK] = res[h * tq:(h + 1) * tq, :]


def mla_prompt(q_lat, q_rope, b_rows, nb, sp):
    tq = _pick_tile(sp, 128, 8)
    tk = _pick_tile(sp, 256, 8)
    nq = sp // tq
    m = B_HEADS * tq
    tok = lambda w: pl.BlockSpec((tq, w), lambda b, i: (b * nq + i, 0))
    return pl.pallas_call(
        functools.partial(_mla_p_body, tq=tq, tk=tk), grid=(nb, nq),
        in_specs=[tok(B_HEADS * B_KVRANK), tok(B_HEADS * LANE),
                  pl.BlockSpec((sp, B_KVRANK + B_ROPE), lambda b, i: (b, 0))],
        out_specs=tok(B_HEADS * B_KVRANK),
        out_shape=jax.ShapeDtypeStruct((nb * sp, B_HEADS * B_KVRANK), F32),
        scratch_shapes=[pltpu.VMEM((m, B_KVRANK), BF), pltpu.VMEM((m, B_ROPE), BF),
                        pltpu.VMEM((m, 1), F32), pltpu.VMEM((m, LANE), F32), pltpu.VMEM((m, B_KVRANK), F32)],
        compiler_params=_cparams("parallel", "arbitrary"), name="mla_prompt",
    )(q_lat, q_rope, b_rows)


def _mla_s_body(pt_ref, ql_ref, qr_ref, new_ref, *rest, g_pages, steps, ss):
    pages = rest[:g_pages]
    o_ref, qls, qrs, m_ref, l_ref, acc_ref = rest[g_pages:]
    nh = B_HEADS
    step = pl.program_id(1)

    @pl.when(step == 0)
    def _():
        for h in range(nh):
            qls[h * ss:(h + 1) * ss, :] = ql_ref[:, h * B_KVRANK:(h + 1) * B_KVRANK]
            qrs[h * ss:(h + 1) * ss, :] = qr_ref[:, h * LANE:(h + 1) * LANE]
        _softmax_init(m_ref, l_ref, acc_ref)

    ql = qls[...].astype(BF)
    qr = qrs[:, 0:B_ROPE].astype(BF)
    ckv_t = _cat_pages_t(pages, 0, B_KVRANK).astype(BF)
    kr_t = _cat_pages_t(pages, B_KVRANK, B_KVRANK + B_ROPE).astype(BF)
    s = (_dot(ql, ckv_t) + _dot(qr, kr_t)) * B_SCALE
    _softmax_step(s, ckv_t, m_ref, l_ref, acc_ref, v_is_transposed=True)

    @pl.when(step == steps - 1)
    def _():
        s_new, ckv_new = _mla_scores(ql, qr, new_ref[...])
        qrow = lax.broadcasted_iota(jnp.int32, (ss, ss), 0)
        kcol = lax.broadcasted_iota(jnp.int32, (ss, ss), 1)
        _softmax_step(_add_head_mask(s_new, jnp.where(kcol <= qrow, 0.0, NEG), nh), ckv_new, m_ref, l_ref, acc_ref)
        res = _softmax_finish(l_ref, acc_ref)
        for h in range(nh):
            o_ref[:, h * B_KVRANK:(h + 1) * B_KVRANK] = res[h * ss:(h + 1) * ss, :]


def mla_sample(q_lat, q_rope, b_rows, pool, page_table, tp, ns, ss):
    n_pages = page_table.shape[1]
    g_pages = _pages_per_step(n_pages)
    steps = n_pages // g_pages
    m = B_HEADS * ss
    row_w = B_KVRANK + B_ROPE
    tok = lambda w: pl.BlockSpec((ss, w), lambda b, s, pt: (tp // ss + b, 0))
    grid_spec = pltpu.PrefetchScalarGridSpec(
        num_scalar_prefetch=1, grid=(ns, steps),
        in_specs=[tok(B_HEADS * B_KVRANK), tok(B_HEADS * LANE), tok(row_w)] + _page_specs_t(g_pages, row_w, 0),
        out_specs=pl.BlockSpec((ss, B_HEADS * B_KVRANK), lambda b, s, pt: (b, 0)),
        scratch_shapes=[pltpu.VMEM((m, B_KVRANK), F32), pltpu.VMEM((m, LANE), F32),
                        pltpu.VMEM((m, 1), F32), pltpu.VMEM((m, LANE), F32), pltpu.VMEM((m, B_KVRANK), F32)])
    return pl.pallas_call(
        functools.partial(_mla_s_body, g_pages=g_pages, steps=steps, ss=ss), grid_spec=grid_spec,
        out_shape=jax.ShapeDtypeStruct((ns * ss, B_HEADS * B_KVRANK), F32),
        compiler_params=_cparams("parallel", "arbitrary"), name="mla_sample",
    )(page_table, q_lat, q_rope, b_rows, *([jnp.swapaxes(pool, 1, 2)] * g_pages))


def even_mixers(h, pos, tp, nb, sp, ns, ss, pool_a, pool_b, page_table, g_cq, g_ckv, w_uq, w_ukv):
    tab_a = _rope_tables(pos, A_DIM // 4, ROPE_THETA, A_DIM)
    tab_b = _rope_tables(pos, B_ROPE, B_ROPE_THETA, B_ROPE)
    aq, a_rows, cqn, b_rows = prep_even(h, tab_a, tab_b, g_cq, g_ckv)
    w_uq_cols = jnp.concatenate([w_uq[:, :, :B_NOPE].reshape(B_QRANK, -1),
                                 w_uq[:, :, B_NOPE:].reshape(B_QRANK, -1)], axis=1).astype(BF)
    q = matmul(cqn, w_uq_cols)
    wuk_t = jnp.transpose(w_ukv[:, :, :B_NOPE], (1, 2, 0)).astype(BF)
    wuv = jnp.transpose(w_ukv[:, :, B_NOPE:], (1, 0, 2)).astype(BF)
    q_lat, q_rope = mla_qprep(q, wuk_t, tab_b)
    o_a = jnp.concatenate([moba_prompt(aq, a_rows, nb, sp),
                           moba_sample(aq, a_rows, pool_a, page_table, tp, ns, ss)], axis=0)
    o_lat = jnp.concatenate([mla_prompt(q_lat, q_rope, b_rows, nb, sp),
                             mla_sample(q_lat, q_rope, b_rows, pool_b, page_table, tp, ns, ss)], axis=0)
    o_b = mla_oproj(o_lat, wuv)
    return jnp.concatenate([o_a, o_b], axis=1), a_rows, b_rows


_O_CK, _O_CV, _O_IQ, _O_IK, _O_IW, _O_DQ, _O_DKV, _O_DG, _O_END = 2048, 2176, 2304, 4352, 4480, 4608, 6656, 7040, 7168
INT_MIN = -2 ** 31


def _odd_weight_cols(w):
    cuts = [int(c) for c in np.cumsum(ODD_SPLITS)[:-1]]
    cq, ck, cv, iq, ik, iw, dq, dkv, dg = jnp.split(w, cuts, axis=1)
    dg = dg.reshape(-1, D_HEADS, 3).transpose(0, 2, 1).reshape(-1, 3 * D_HEADS)
    pad = lambda a, n: jnp.pad(a, ((0, 0), (0, n - a.shape[1])))
    return jnp.concatenate([cq, ck, cv, iq, pad(ik, LANE), pad(iw, LANE), dq, dkv, pad(dg, LANE)], axis=1)


def _split_heads_64(r, lane):
    return jnp.where(lane < 64, r, 0.0), jnp.where(lane < 64, pltpu.roll(r, 64, 1), 0.0)


def _prep_odd_body(h_ref, ca, sma, spa, ci, smi, spi, cq_ref, crows_ref, iq_ref, iw_ref, dq_ref, drows_ref,
                   win_ref, gate_ref):
    c, sm, sp = ca[...], sma[...], spa[...]
    ha = C_DIM // 8
    for h in range(C_HEADS):
        cq_ref[:, h * LANE:(h + 1) * LANE] = _rope(h_ref[:, h * LANE:(h + 1) * LANE], c, sm, sp, ha)
    crows_ref[:, 0:C_DIM] = _rope(h_ref[:, _O_CK:_O_CK + C_DIM], c, sm, sp, ha)
    crows_ref[:, C_DIM:2 * C_DIM] = h_ref[:, _O_CV:_O_CV + C_DIM]
    c, sm, sp = ci[...], smi[...], spi[...]
    hi = C_IDX_DIM // 8
    lane = lax.broadcasted_iota(jnp.int32, c.shape, 1)
    ik = _rope(h_ref[:, _O_IK:_O_IK + LANE], c, sm, sp, hi)
    crows_ref[:, 2 * C_DIM:2 * C_DIM + C_IDX_DIM] = ik[:, 0:C_IDX_DIM]
    for j in range(C_IDX_HEADS // 2):
        a, b = _split_heads_64(_rope(h_ref[:, _O_IQ + j * LANE:_O_IQ + (j + 1) * LANE], c, sm, sp, hi), lane)
        iq_ref[:, (2 * j) * LANE:(2 * j + 1) * LANE] = a
        iq_ref[:, (2 * j + 1) * LANE:(2 * j + 2) * LANE] = b
    iw_ref[...] = h_ref[:, _O_IW:_O_IW + LANE] * C_IDX_W_SCALE
    for j in range(D_HEADS // 2):
        a, b = _split_heads_64(_rope(h_ref[:, _O_DQ + j * LANE:_O_DQ + (j + 1) * LANE], c, sm, sp, hi) * D_SCALE, lane)
        dq_ref[:, (2 * j) * LANE:(2 * j + 1) * LANE] = a
        dq_ref[:, (2 * j + 1) * LANE:(2 * j + 2) * LANE] = b
    left = lane < D_DIM
    c1, sm1, sp1 = jnp.where(left, c, 1.0), jnp.where(left, sm, 0.0), jnp.where(left, sp, 0.0)
    drows_ref[:, 0:LANE] = h_ref[:, _O_DKV:_O_DKV + LANE]
    drows_ref[:, LANE:2 * LANE] = _rope(h_ref[:, _O_DKV + LANE:_O_DKV + 2 * LANE], c1, sm1, sp1, hi)
    win_ref[...] = _rope(h_ref[:, _O_DKV + 2 * LANE:_O_DKV + 3 * LANE], c1, sm1, sp1, hi)
    gate_ref[...] = jax.nn.sigmoid(h_ref[:, _O_DG:_O_DG + LANE])


def prep_odd(h, tab_a, tab_i):
    t, w = h.shape
    tm = _pick_tile(t, 256, 8)
    tab = _tok_spec(tm, LANE)
    widths = [C_HEADS * C_DIM, 2 * C_DIM + C_IDX_DIM, C_IDX_HEADS * LANE, LANE, D_HEADS * LANE, 4 * D_DIM, LANE, LANE]
    return pl.pallas_call(
        _prep_odd_body, grid=(t // tm,),
        in_specs=[_tok_spec(tm, w), tab, tab, tab, tab, tab, tab],
        out_specs=[_tok_spec(tm, wd) for wd in widths],
        out_shape=[jax.ShapeDtypeStruct((t, wd), F32) for wd in widths],
        compiler_params=_cparams("parallel"), name="prep_odd",
    )(h, *tab_a, *tab_i)


def _sort_key(score, adm):
    b = lax.bitcast_convert_type(score + 0.0, jnp.int32)
    key = b ^ ((b >> 31) & jnp.int32(0x7FFFFFFF))
    return jnp.where(adm, key, jnp.int32(INT_MIN))


def _kth_largest(count_ge, shape, k):
    def body(it, ans):
        cand = ans + jnp.left_shift(jnp.int32(1), 31 - it)
        return jnp.where(count_ge(cand) >= k, cand, ans)
    return lax.fori_loop(0, 32, body, jnp.full(shape, INT_MIN, jnp.int32))


def _index_scores(iq_heads, iw, ikt):
    score = None
    for h in range(C_IDX_HEADS):
        rel = jnp.maximum(_dot_nt(iq_heads(h), ikt), 0.0)
        term = iw[:, h:h + 1] * rel
        score = term if score is None else score + term
    return score


def _tie_prefix(tie):
    r = lax.broadcasted_iota(jnp.int32, (LANE, LANE), 0)
    c = lax.broadcasted_iota(jnp.int32, (LANE, LANE), 1)
    upper = jnp.where(r < c, 1.0, 0.0).astype(BF)
    return _dot(jnp.where(tie, 1.0, 0.0).astype(BF), upper)


def _dsa_p_body(cq_ref, iq_ref, iw_ref, rows_ref, o_ref, qs, sck, m_ref, l_ref, acc_ref, *, tq, tk, sp, kk):
    nh = C_HEADS
    qi = pl.program_id(1)
    q0 = qi * tq
    ntile = sp // tk
    for h in range(nh):
        qs[h * tq:(h + 1) * tq, :] = cq_ref[:, h * C_DIM:(h + 1) * C_DIM].astype(BF)
    iw = iw_ref[...]
    qpos = q0 + lax.broadcasted_iota(jnp.int32, (tq, tk), 0)
    lane_k = lax.broadcasted_iota(jnp.int32, (tq, tk), 1)
    for j in range(ntile):
        @pl.when(j * tk < q0 + tq)
        def _():
            ikt = rows_ref[j * tk:(j + 1) * tk, 2 * C_DIM:2 * C_DIM + C_IDX_DIM].astype(BF)
            score = _index_scores(lambda h: iq_ref[:, h * LANE:h * LANE + C_IDX_DIM].astype(BF), iw, ikt)
            sck[:, j * tk:(j + 1) * tk] = _sort_key(score, j * tk + lane_k <= qpos)

        @pl.when(j * tk >= q0 + tq)
        def _():
            sck[:, j * tk:(j + 1) * tk] = jnp.full((tq, tk), INT_MIN, jnp.int32)

    def count_ge(t):
        return jnp.sum(jnp.where(sck[...] >= t, 1.0, 0.0), axis=-1, keepdims=True)

    thr = _kth_largest(count_ge, (tq, 1), float(kk))
    n_gt = jnp.sum(jnp.where(sck[...] > thr, 1.0, 0.0), axis=-1, keepdims=True)
    need = float(kk) - n_gt
    excess = (count_ge(thr) - n_gt > need) & (thr > INT_MIN)

    @pl.when(jnp.max(jnp.where(excess, 1.0, 0.0)) > 0.0)
    def _():
        carry = jnp.zeros((tq, 1), F32)
        for c in range(sp // LANE):
            keys = sck[:, c * LANE:(c + 1) * LANE]
            tie = keys == thr
            drop = excess & tie & (carry + _tie_prefix(tie) >= need)
            sck[:, c * LANE:(c + 1) * LANE] = jnp.where(drop, jnp.int32(INT_MIN), keys)
            carry = carry + jnp.sum(jnp.where(tie, 1.0, 0.0), axis=-1, keepdims=True)

    thr = jnp.maximum(thr, INT_MIN + 1)
    _softmax_init(m_ref, l_ref, acc_ref)
    q = qs[...]
    for j in range(ntile):
        @pl.when(j * tk < q0 + tq)
        def _():
            k = rows_ref[j * tk:(j + 1) * tk, 0:C_DIM].astype(BF)
            v = rows_ref[j * tk:(j + 1) * tk, C_DIM:2 * C_DIM].astype(BF)
            maskf = jnp.where(sck[:, j * tk:(j + 1) * tk] >= thr, 0.0, NEG)
            _softmax_step(_add_head_mask(_dot_nt(q, k) * C_SCALE, maskf, nh), v, m_ref, l_ref, acc_ref)
    res = _softmax_finish(l_ref, acc_ref)
    for h in range(nh):
        o_ref[:, h * C_DIM:(h + 1) * C_DIM] = res[h * tq:(h + 1) * tq, :].astype(o_ref.dtype)


def dsa_prompt(cq, iq, iw, c_rows, nb, sp):
    tq = _pick_tile(sp, 128, 8)
    tk = _pick_tile(sp, 512, LANE)
    nq = sp // tq
    m = C_HEADS * tq
    tok = lambda w: pl.BlockSpec((tq, w), lambda b, i: (b * nq + i, 0))
    return pl.pallas_call(
        functools.partial(_dsa_p_body, tq=tq, tk=tk, sp=sp, kk=min(C_TOPK, sp // 4)), grid=(nb, nq),
        in_specs=[tok(C_HEADS * C_DIM), tok(C_IDX_HEADS * LANE), tok(LANE),
                  pl.BlockSpec((sp, 2 * C_DIM + C_IDX_DIM), lambda b, i: (b, 0))],
        out_specs=tok(C_HEADS * C_DIM),
        out_shape=jax.ShapeDtypeStruct((nb * sp, C_HEADS * C_DIM), BF),
        scratch_shapes=[pltpu.VMEM((m, C_DIM), BF), pltpu.VMEM((tq, sp), jnp.int32),
                        pltpu.VMEM((m, 1), F32), pltpu.VMEM((m, LANE), F32), pltpu.VMEM((m, C_DIM), F32)],
        compiler_params=_cparams("parallel", "arbitrary"), name="dsa_prompt",
    )(cq, iq, iw, c_rows)


def _dsa_si_body(pt_ref, iq_ref, iw_ref, new_ref, *rest, g_pages, steps, ss, kk):
    pages = rest[:g_pages]
    mask_ref, iqs, sc = rest[g_pages:]
    step = pl.program_id(1)
    w = g_pages * PAGE_SIZE
    nhi = C_IDX_HEADS

    @pl.when(step == 0)
    def _():
        for h in range(nhi):
            iqs[h * ss:(h + 1) * ss, :] = iq_ref[:, h * LANE:(h + 1) * LANE]

    iw = iw_ref[...]
    iq_all = iqs[:, 0:C_IDX_DIM].astype(BF)

    def scores(rel):
        acc = iw[:, 0:1] * rel[0:ss]
        for h in range(1, nhi):
            acc = acc + iw[:, h:h + 1] * rel[h * ss:(h + 1) * ss]
        return acc

    ik_t = _cat_pages_t(pages, 0, C_IDX_DIM).astype(BF)
    sc[step] = _sort_key(scores(jnp.maximum(_dot(iq_all, ik_t), 0.0)), jnp.full((ss, w), True))

    @pl.when(step == steps - 1)
    def _():
        ikn = new_ref[:, 2 * C_DIM:2 * C_DIM + C_IDX_DIM].astype(BF)
        s_new = scores(jnp.maximum(_dot_nt(iq_all, ikn), 0.0))
        qrow = lax.broadcasted_iota(jnp.int32, (ss, ss), 0)
        kcol = lax.broadcasted_iota(jnp.int32, (ss, ss), 1)
        sc[steps] = jnp.full((ss, w), INT_MIN, jnp.int32)
        sc[steps, :, 0:ss] = _sort_key(s_new, kcol <= qrow)

        def count_ge(t):
            return jnp.sum(jnp.sum(jnp.where(sc[...] >= t[None], 1.0, 0.0), axis=0), axis=-1, keepdims=True)

        thr = _kth_largest(count_ge, (ss, 1), float(kk))
        n_gt = jnp.sum(jnp.sum(jnp.where(sc[...] > thr[None], 1.0, 0.0), axis=0), axis=-1, keepdims=True)
        need = float(kk) - n_gt
        excess = (count_ge(thr) - n_gt > need) & (thr > INT_MIN)

        @pl.when(jnp.max(jnp.where(excess, 1.0, 0.0)) > 0.0)
        def _():
            def slot(s, carry):
                for c in range(w // LANE):
                    keys = sc[s, :, c * LANE:(c + 1) * LANE]
                    tie = keys == thr
                    drop = excess & tie & (carry + _tie_prefix(tie) >= need)
                    sc[s, :, c * LANE:(c + 1) * LANE] = jnp.where(drop, jnp.int32(INT_MIN), keys)
                    carry = carry + jnp.sum(jnp.where(tie, 1.0, 0.0), axis=-1, keepdims=True)
                return carry
            lax.fori_loop(0, steps + 1, slot, jnp.zeros((ss, 1), F32))

        thr2 = jnp.maximum(thr, INT_MIN + 1)
        mask_ref[...] = jnp.where(sc[...] >= thr2[None], 0.0, NEG)


def _dsa_sa_body(pt_ref, cq_ref, new_ref, mask_ref, mask_new_ref, *rest, g_pages, steps, ss):
    pages = rest[:g_pages]
    o_ref, qs, m_ref, l_ref, acc_ref = rest[g_pages:]
    nh = C_HEADS
    step = pl.program_id(1)

    @pl.when(step == 0)
    def _():
        for h in range(nh):
            qs[h * ss:(h + 1) * ss, :] = cq_ref[:, h * C_DIM:(h + 1) * C_DIM]
        _softmax_init(m_ref, l_ref, acc_ref)

    q = qs[...].astype(BF)
    k_t = _cat_pages_t(pages, 0, C_DIM).astype(BF)
    v_t = _cat_pages_t(pages, C_DIM, 2 * C_DIM).astype(BF)
    _softmax_step(_add_head_mask(_dot(q, k_t) * C_SCALE, mask_ref[...], nh), v_t, m_ref, l_ref, acc_ref,
                  v_is_transposed=True)

    @pl.when(step == steps - 1)
    def _():
        kn = new_ref[:, 0:C_DIM].astype(BF)
        vn = new_ref[:, C_DIM:2 * C_DIM].astype(BF)
        s_new = _add_head_mask(_dot_nt(q, kn) * C_SCALE, mask_new_ref[:, 0:ss], nh)
        _softmax_step(s_new, vn, m_ref, l_ref, acc_ref)
        res = _softmax_finish(l_ref, acc_ref)
        for h in range(nh):
            o_ref[:, h * C_DIM:(h + 1) * C_DIM] = res[h * ss:(h + 1) * ss, :].astype(o_ref.dtype)


def dsa_sample(cq, iq, iw, c_rows, pool, page_table, tp, ns, ss):
    n_pages = page_table.shape[1]
    g_pages = _pages_per_step(n_pages)
    steps = n_pages // g_pages
    w = g_pages * PAGE_SIZE
    row_w = 2 * C_DIM + C_IDX_DIM
    kk = min(C_TOPK, (n_pages * PAGE_SIZE + ss) // 4)
    tok = lambda wd: pl.BlockSpec((ss, wd), lambda b, s, pt: (tp // ss + b, 0))
    pool_t = jnp.swapaxes(pool, 1, 2)
    mask = pl.pallas_call(
        functools.partial(_dsa_si_body, g_pages=g_pages, steps=steps, ss=ss, kk=kk),
        grid_spec=pltpu.PrefetchScalarGridSpec(
            num_scalar_prefetch=1, grid=(ns, steps),
            in_specs=[tok(C_IDX_HEADS * LANE), tok(LANE), tok(row_w)]
            + _page_specs_t(g_pages, C_IDX_DIM, 2 * C_DIM // C_IDX_DIM),
            out_specs=pl.BlockSpec((None, steps + 1, ss, w), lambda b, s, pt: (b, 0, 0, 0)),
            scratch_shapes=[pltpu.VMEM((C_IDX_HEADS * ss, LANE), F32), pltpu.VMEM((steps + 1, ss, w), jnp.int32)]),
        out_shape=jax.ShapeDtypeStruct((ns, steps + 1, ss, w), F32),
        compiler_params=_cparams("parallel", "arbitrary"), name="dsa_sample_index",
    )(page_table, iq, iw, c_rows, *([pool_t] * g_pages))
    m = C_HEADS * ss
    return pl.pallas_call(
        functools.partial(_dsa_sa_body, g_pages=g_pages, steps=steps, ss=ss),
        grid_spec=pltpu.PrefetchScalarGridSpec(
            num_scalar_prefetch=1, grid=(ns, steps),
            in_specs=[tok(C_HEADS * C_DIM), tok(row_w),
                      pl.BlockSpec((None, None, ss, w), lambda b, s, pt: (b, s, 0, 0)),
                      pl.BlockSpec((None, None, ss, w), lambda b, s, pt: (b, steps, 0, 0))]
            + _page_specs_t(g_pages, 2 * C_DIM, 0),
            out_specs=pl.BlockSpec((ss, C_HEADS * C_DIM), lambda b, s, pt: (b, 0)),
            scratch_shapes=[pltpu.VMEM((m, C_DIM), F32), pltpu.VMEM((m, 1), F32), pltpu.VMEM((m, LANE), F32),
                            pltpu.VMEM((m, C_DIM), F32)]),
        out_shape=jax.ShapeDtypeStruct((ns * ss, C_HEADS * C_DIM), BF),
        compiler_params=_cparams("parallel", "arbitrary"), name="dsa_sample_attend",
    )(page_table, cq, c_rows, mask, mask, *([pool_t] * g_pages))


_CH = D_CMP_STRIDE
_SLC_PER_CH = D_SLC_BLOCK // D_CMP_STRIDE


def _round_up(n, m):
    return -(-n // m) * m


def _phi_weights(pe, w1, w2):
    nch = D_CMP_LEN // _CH
    halves = []
    for half in range(nch):
        wk = w1[0, half * _CH * D_DIM:(half + 1) * _CH * D_DIM].reshape(_CH, D_DIM, -1)
        wv = w1[1, half * _CH * D_DIM:(half + 1) * _CH * D_DIM].reshape(_CH, D_DIM, -1)
        z = jnp.zeros_like(wk)
        blk = jnp.concatenate([jnp.concatenate([wk, z], axis=2), jnp.concatenate([z, wv], axis=2)], axis=1)
        halves.append(blk.reshape(_CH * 2 * D_DIM, -1).astype(BF))
    pes = [jnp.concatenate([pe[0, half * _CH:(half + 1) * _CH], pe[1, half * _CH:(half + 1) * _CH]],
                           axis=1).reshape(1, -1) for half in range(nch)]
    z2 = jnp.zeros_like(w2[0])
    w2b = jnp.concatenate([jnp.concatenate([w2[0], z2], axis=1), jnp.concatenate([z2, w2[1]], axis=1)], axis=0)
    return jnp.concatenate(halves, axis=1), pes[0], pes[1], w2b.astype(BF)


def _chunk_rows(read, n_chunks):
    return jnp.concatenate([read(j, n_chunks) for j in range(_CH)], axis=1)


def _compress_pre(x, wcat, pelo, pehi):
    y = _dot(x.astype(BF), wcat)
    pe = jnp.concatenate([jnp.broadcast_to(pelo, (4, pelo.shape[1])), jnp.broadcast_to(pehi, (4, pehi.shape[1]))], axis=0)
    bias = _dot(pe.astype(BF), wcat)
    lo = y[:, 0:LANE] + bias[0:1, 0:LANE]
    hi = y[:, LANE:2 * LANE] + bias[4:5, LANE:2 * LANE]
    return lo, hi


def _compress_post(lo, hi, w2b):
    n = lo.shape[0]
    return _dot(jax.nn.silu(lo + pltpu.roll(hi, n - 1, 0)).astype(BF), w2b)


def _cmp_branch(q, kvc, qpos, n_tok, nh, tq):
    ncp = kvc.shape[0]
    c_idx = lax.broadcasted_iota(jnp.int32, (tq, ncp), 1)
    vis = (c_idx < n_tok) & (c_idx * D_CMP_STRIDE + (D_CMP_LEN - 1) <= qpos)
    s = _add_head_mask(_dot_nt(q, kvc), jnp.where(vis, 0.0, NEG), nh)
    m = jnp.maximum(jnp.max(s, axis=-1, keepdims=True), M_FLOOR)
    p = jnp.exp(s - m)
    l = jnp.sum(p, axis=-1, keepdims=True)
    l = jnp.where(l > 0.0, l, 1.0)
    o = _dot(p.astype(BF), kvc) / l
    pn = p / l
    psum = pn[0:tq]
    for h in range(1, nh):
        psum = psum + pn[h * tq:(h + 1) * tq]
    return o, psum


def _block_importance(psum, nsbp):
    ncp = psum.shape[1]
    c = lax.broadcasted_iota(jnp.int32, (ncp, nsbp), 0)
    j = lax.broadcasted_iota(jnp.int32, (ncp, nsbp), 1)
    rc = D_CMP_LEN // D_CMP_STRIDE
    band = jnp.where((c >= _SLC_PER_CH * j - (rc - 1)) & (c <= _SLC_PER_CH * j + _SLC_PER_CH - 1), 1.0, 0.0).astype(BF)
    hi = psum.astype(BF)
    r1 = psum - hi.astype(F32)
    mid = r1.astype(BF)
    lo = (r1 - mid.astype(F32)).astype(BF)
    return _dot(hi, band) + _dot(mid, band) + _dot(lo, band)


def _select_blocks(imp, qpos_col, nsb):
    jb = lax.broadcasted_iota(jnp.int32, imp.shape, 1)
    bt = qpos_col // D_SLC_BLOCK
    forced = (jb == 0) | (jb >= bt - (D_SLC_LOCAL - 1))
    val = jnp.where(forced, jnp.inf, imp)
    sel = _top_lanes(val, (jb <= bt) & (jb < nsb), min(D_SLC_TOPN, nsb))
    return jnp.where(sel, 1.0, 0.0)


def _slc_mask(selm, k0, tk, qpos):
    nsbp = selm.shape[1]
    jb = lax.broadcasted_iota(jnp.int32, (nsbp, tk), 0)
    kk = lax.broadcasted_iota(jnp.int32, (nsbp, tk), 1)
    expand = jnp.where((k0 + kk) // D_SLC_BLOCK == jb, 1.0, 0.0).astype(BF)
    chosen = _dot(selm.astype(BF), expand) > 0.5
    kpos = k0 + lax.broadcasted_iota(jnp.int32, (selm.shape[0], tk), 1)
    return jnp.where(chosen & (kpos <= qpos), 0.0, NEG)


def _nsa_combine(o_cmp, o_slc, o_win, gates, nh, tq, o_ref):
    lane = lax.broadcasted_iota(jnp.int32, (tq, LANE), 1)
    outs = []
    for h in range(nh):
        r = slice(h * tq, (h + 1) * tq)
        outs.append(gates[:, h:h + 1] * o_cmp[r] + gates[:, nh + h:nh + h + 1] * o_slc[r]
                    + gates[:, 2 * nh + h:2 * nh + h + 1] * o_win[r])
    for j in range(nh // 2):
        o_ref[:, j * LANE:(j + 1) * LANE] = jnp.where(lane < D_DIM, pltpu.roll(outs[2 * j], D_DIM, 1),
                                                      outs[2 * j + 1]).astype(o_ref.dtype)


def _nsa_p_body(dq_ref, gate_ref, cmp_ref, slc_ref, win_ref, wcat_ref, pelo_ref, pehi_ref, w2_ref, o_ref,
                qs, kvc, m_ref, l_ref, acc_ref, m2_ref, l2_ref, acc2_ref, *, tq, tk, sp):
    nh = D_HEADS
    qi = pl.program_id(1)
    q0 = qi * tq
    n_ch = sp // _CH
    ncp = kvc.shape[0]
    nsb = sp // D_SLC_BLOCK
    nsbp = _round_up(nsb, LANE)

    @pl.when(qi == 0)
    def _():
        x = _chunk_rows(lambda j, n: cmp_ref[pl.ds(j, n, stride=_CH), :], n_ch)
        lo, hi = _compress_pre(x, wcat_ref[...], pelo_ref[...], pehi_ref[...])
        kvc[...] = jnp.zeros(kvc.shape, kvc.dtype)
        kvc[0:n_ch, :] = _compress_post(lo, hi, w2_ref[...]).astype(kvc.dtype)

    for h in range(nh):
        qs[h * tq:(h + 1) * tq, :] = dq_ref[:, h * LANE:(h + 1) * LANE].astype(BF)
    q = qs[...]
    qpos_col = q0 + lax.broadcasted_iota(jnp.int32, (tq, 1), 0)
    o_cmp, psum = _cmp_branch(q, kvc[...], qpos_col, n_ch - 1, nh, tq)
    selm = _select_blocks(_block_importance(psum, nsbp), qpos_col, nsb)
    _softmax_init(m_ref, l_ref, acc_ref)
    for j in range(sp // tk):
        @pl.when(j * tk < q0 + tq)
        def _():
            kv = slc_ref[j * tk:(j + 1) * tk, :].astype(BF)
            maskf = _slc_mask(selm, j * tk, tk, qpos_col)
            _softmax_step(_add_head_mask(_dot_nt(q, kv), maskf, nh), kv, m_ref, l_ref, acc_ref)
    o_slc = _softmax_finish(l_ref, acc_ref)
    _softmax_init(m2_ref, l2_ref, acc2_ref)
    base = (q0 // tk) * tk
    for w in range(D_WINDOW // tk + 1):
        k0 = base - D_WINDOW + w * tk

        @pl.when(k0 >= 0)
        def _():
            kv = win_ref[pl.ds(pl.multiple_of(k0, tk), tk), :].astype(BF)
            dist = qpos_col - (k0 + lax.broadcasted_iota(jnp.int32, (tq, tk), 1))
            maskf = jnp.where((dist >= 0) & (dist < D_WINDOW), 0.0, NEG)
            _softmax_step(_add_head_mask(_dot_nt(q, kv), maskf, nh), kv, m2_ref, l2_ref, acc2_ref)
    o_win = _softmax_finish(l2_ref, acc2_ref)
    _nsa_combine(o_cmp, o_slc, o_win, gate_ref[...], nh, tq, o_ref)


def nsa_prompt(dq, gates, d_rows, win_rows, phi, nb, sp):
    tq = _pick_tile(sp, 128, 8)
    tk = _pick_tile(sp, 256, LANE)
    assert D_WINDOW % tk == 0 and tk % tq == 0
    nq = sp // tq
    m = D_HEADS * tq
    ncp = _round_up(sp // _CH, LANE)
    tok = lambda w: pl.BlockSpec((tq, w), lambda b, i: (b * nq + i, 0))
    seq = lambda col: pl.BlockSpec((sp, LANE), lambda b, i: (b, col))
    stats = [pltpu.VMEM((m, 1), F32), pltpu.VMEM((m, LANE), F32), pltpu.VMEM((m, LANE), F32)]
    return pl.pallas_call(
        functools.partial(_nsa_p_body, tq=tq, tk=tk, sp=sp), grid=(nb, nq),
        in_specs=[tok(D_HEADS * LANE), tok(LANE), seq(0), seq(1), seq(0)] + [_full_spec(a.shape) for a in phi],
        out_specs=tok(D_HEADS * D_DIM),
        out_shape=jax.ShapeDtypeStruct((nb * sp, D_HEADS * D_DIM), BF),
        scratch_shapes=[pltpu.VMEM((m, LANE), BF), pltpu.VMEM((ncp, LANE), BF)] + stats + stats,
        compiler_params=_cparams("parallel", "arbitrary"), name="nsa_prompt",
    )(dq, gates, d_rows, d_rows, win_rows, *phi)


def _nsa_sc_body(pt_ref, dq_ref, wcat_ref, pelo_ref, pehi_ref, w2_ref, *rest,
                 g_pages, steps, ss, past_len, nsb):
    pages = rest[:g_pages]
    ocmp_ref, selm_ref, qs, los, his = rest[g_pages:]
    nh = D_HEADS
    step = pl.program_id(1)
    cpp = PAGE_SIZE // _CH
    n_ch = past_len // _CH

    @pl.when(step == 0)
    def _():
        for h in range(nh):
            qs[h * ss:(h + 1) * ss, :] = dq_ref[:, h * LANE:(h + 1) * LANE]

    x = jnp.concatenate([_chunk_rows(lambda j, n: p[pl.ds(j, n, stride=_CH), :], cpp) for p in pages], axis=0)
    lo, hi = _compress_pre(x, wcat_ref[...], pelo_ref[...], pehi_ref[...])
    r0 = pl.multiple_of(step * (g_pages * cpp), g_pages * cpp)
    los[pl.ds(r0, g_pages * cpp), :] = lo
    his[pl.ds(r0, g_pages * cpp), :] = hi

    @pl.when(step == steps - 1)
    def _():
        kvc = _compress_post(los[...], his[...], w2_ref[...]).astype(BF)
        qpos_col = past_len + lax.broadcasted_iota(jnp.int32, (ss, 1), 0)
        o_cmp, psum = _cmp_branch(qs[...].astype(BF), kvc, qpos_col, n_ch - 1, nh, ss)
        ocmp_ref[...] = o_cmp
        selm_ref[...] = _select_blocks(_block_importance(psum, selm_ref.shape[-1]), qpos_col, nsb)


def _nsa_sa_body(pt_ref, dq_ref, gate_ref, new_ref, wnew_ref, wstate_ref, ocmp_ref, selm_ref, *rest,
                 g_pages, steps, ss, past_len):
    pages = rest[:g_pages]
    o_ref, qs, m_ref, l_ref, acc_ref = rest[g_pages:]
    nh = D_HEADS
    step = pl.program_id(1)
    w = g_pages * PAGE_SIZE

    @pl.when(step == 0)
    def _():
        for h in range(nh):
            qs[h * ss:(h + 1) * ss, :] = dq_ref[:, h * LANE:(h + 1) * LANE]
        _softmax_init(m_ref, l_ref, acc_ref)

    q = qs[...].astype(BF)
    qpos_col = past_len + lax.broadcasted_iota(jnp.int32, (ss, 1), 0)
    selm = selm_ref[...]
    kv = jnp.concatenate([p[...] for p in pages], axis=0).astype(BF)
    maskf = _slc_mask(selm, step * w, w, qpos_col)
    _softmax_step(_add_head_mask(_dot_nt(q, kv), maskf, nh), kv, m_ref, l_ref, acc_ref)

    @pl.when(step == steps - 1)
    def _():
        kvn = new_ref[:, LANE:2 * LANE].astype(BF)
        maskn = _slc_mask(selm, past_len, ss, qpos_col)
        _softmax_step(_add_head_mask(_dot_nt(q, kvn), maskn, nh), kvn, m_ref, l_ref, acc_ref)
        o_slc = _softmax_finish(l_ref, acc_ref)
        _softmax_init(m_ref, l_ref, acc_ref)
        wbuf = wstate_ref.shape[0]
        for kvw, k0 in ((wstate_ref[...].astype(BF), past_len - wbuf), (wnew_ref[...].astype(BF), past_len)):
            n = kvw.shape[0]
            dist = qpos_col - (k0 + lax.broadcasted_iota(jnp.int32, (ss, n), 1))
            maskw = jnp.where((dist >= 0) & (dist < D_WINDOW), 0.0, NEG)
            _softmax_step(_add_head_mask(_dot_nt(q, kvw), maskw, nh), kvw, m_ref, l_ref, acc_ref)
        o_win = _softmax_finish(l_ref, acc_ref)
        _nsa_combine(ocmp_ref[...], o_slc, o_win, gate_ref[...], nh, ss, o_ref)


def nsa_sample(dq, gates, d_rows, win_rows, win_state, phi, pool, page_table, tp, ns, ss):
    n_pages = page_table.shape[1]
    g_pages = _pages_per_step(n_pages)
    steps = n_pages // g_pages
    past_len = n_pages * PAGE_SIZE
    assert ss <= D_CMP_STRIDE and past_len % D_SLC_BLOCK == 0
    n_ch = past_len // _CH
    nsb = -(-(past_len + ss) // D_SLC_BLOCK)
    nsbp = _round_up(nsb, LANE)
    m = D_HEADS * ss
    tok = lambda wd: pl.BlockSpec((ss, wd), lambda b, s, pt: (tp // ss + b, 0))
    per_seq = lambda shape: pl.BlockSpec((None,) + shape, lambda b, s, pt: (b,) + (0,) * len(shape))
    phi_specs = [pl.BlockSpec(a.shape, lambda b, s, pt, nd=a.ndim: (0,) * nd) for a in phi]
    gc_pages = _pages_per_step(n_pages, 32)
    o_cmp, selm = pl.pallas_call(
        functools.partial(_nsa_sc_body, g_pages=gc_pages, steps=n_pages // gc_pages, ss=ss, past_len=past_len,
                          nsb=nsb),
        grid_spec=pltpu.PrefetchScalarGridSpec(
            num_scalar_prefetch=1, grid=(ns, n_pages // gc_pages),
            in_specs=[tok(D_HEADS * LANE)] + phi_specs + _page_specs(gc_pages, LANE, 0),
            out_specs=[per_seq((m, LANE)), per_seq((ss, nsbp))],
            scratch_shapes=[pltpu.VMEM((m, LANE), F32), pltpu.VMEM((n_ch, LANE), F32), pltpu.VMEM((n_ch, LANE), F32)]),
        out_shape=[jax.ShapeDtypeStruct((ns, m, LANE), F32), jax.ShapeDtypeStruct((ns, ss, nsbp), F32)],
        compiler_params=_cparams("parallel", "arbitrary"), name="nsa_sample_compress",
    )(page_table, dq, *phi, *([pool] * gc_pages))
    wbuf = win_state.shape[1]
    return pl.pallas_call(
        functools.partial(_nsa_sa_body, g_pages=g_pages, steps=steps, ss=ss, past_len=past_len),
        grid_spec=pltpu.PrefetchScalarGridSpec(
            num_scalar_prefetch=1, grid=(ns, steps),
            in_specs=[tok(D_HEADS * LANE), tok(LANE), tok(4 * D_DIM), tok(LANE), per_seq((wbuf, LANE)),
                      per_seq((m, LANE)), per_seq((ss, nsbp))] + _page_specs(g_pages, LANE, 1),
            out_specs=pl.BlockSpec((ss, D_HEADS * D_DIM), lambda b, s, pt: (b, 0)),
            scratch_shapes=[pltpu.VMEM((m, LANE), F32), pltpu.VMEM((m, 1), F32), pltpu.VMEM((m, LANE), F32),
                            pltpu.VMEM((m, LANE), F32)]),
        out_shape=jax.ShapeDtypeStruct((ns * ss, D_HEADS * D_DIM), BF),
        compiler_params=_cparams("parallel", "arbitrary"), name="nsa_sample_attend",
    )(page_table, dq, gates, d_rows, win_rows, win_state, o_cmp, selm, *([pool] * g_pages))


def odd_mixers(h, pos, tp, nb, sp, ns, ss, pool_c, pool_d, page_table, win_state, phi_pe, phi_w1, phi_w2):
    tab_a = _rope_tables(pos, C_DIM // 4, ROPE_THETA, C_DIM)
    tab_i = _rope_tables(pos, C_IDX_DIM // 4, ROPE_THETA, C_IDX_DIM)
    cq, c_rows, iq, iw, dq, d_rows, win_rows, gates = prep_odd(h, tab_a, tab_i)
    phi = _phi_weights(phi_pe, phi_w1, phi_w2)
    o_c = jnp.concatenate([dsa_prompt(cq, iq, iw, c_rows, nb, sp),
                           dsa_sample(cq, iq, iw, c_rows, pool_c, page_table, tp, ns, ss)], axis=0)
    o_d = jnp.concatenate([nsa_prompt(dq, gates, d_rows, win_rows, phi, nb, sp),
                           nsa_sample(dq, gates, d_rows, win_rows, win_state, phi, pool_d, page_table, tp, ns, ss)],
                          axis=0)
    return jnp.concatenate([o_c, o_d], axis=1), c_rows, d_rows, win_rows


def _pad_cols(w, mult):
    n = w.shape[1]
    npad = -(-n // mult) * mult
    return jnp.pad(w, ((0, 0), (0, npad - n))) if npad != n else w


def kernel(x_prompt, x_sample, cache_a_kv, cache_b_latent, cache_c_kvi, cache_d_kv, state_d_win, page_table,
           w_in_even, b_g_cq, b_g_ckv, b_w_uq, b_w_ukv, w_out_even,
           w_in_odd, d_phi_pe, d_phi_w1, d_phi_w2, w_out_odd,
           mlp_w1, mlp_w2, ln_g, ln_b):
    nb, sp, d = x_prompt.shape
    ns, ss, _ = x_sample.shape
    tp = nb * sp
    past_len = page_table.shape[1] * PAGE_SIZE
    pos_p = jnp.arange(sp, dtype=jnp.int32)
    pos_s = past_len + jnp.arange(ss, dtype=jnp.int32)
    bf = jnp.bfloat16
    x = jnp.concatenate([x_prompt.reshape(tp, d), x_sample.reshape(ns * ss, d)], axis=0)

    def groups(h):
        return h[:tp].reshape(nb, sp, -1), h[tp:].reshape(ns, ss, -1)

    def post(x, o, w_out, layer):
        x, xb = resid_layer_norm(x, matmul(o, w_out.astype(bf)), ln_g[layer, 0], ln_b[layer, 0])
        hm = matmul(xb, mlp_w1[layer].astype(bf), act="relu2", out_dtype=bf)
        return resid_layer_norm(x, matmul(hm, mlp_w2[layer].astype(bf)), ln_g[layer, 1], ln_b[layer, 1])

    pos_all = jnp.concatenate([jnp.tile(pos_p, nb), jnp.tile(pos_s, ns)])
    h0 = matmul(x, _pad_cols(w_in_even, LANE).astype(bf))
    o0, a_rows, b_rows = even_mixers(h0, pos_all, tp, nb, sp, ns, ss, cache_a_kv, cache_b_latent, page_table,
                                     b_g_cq, b_g_ckv, b_w_uq, b_w_ukv)
    a_p, a_s = groups(a_rows)
    b_p, b_s = groups(b_rows)
    x, xb = post(x, o0, w_out_even, 0)
    h1 = matmul(xb, _odd_weight_cols(w_in_odd).astype(bf))
    o1, c_rows, d_rows, win_rows = odd_mixers(h1, pos_all, tp, nb, sp, ns, ss, cache_c_kvi, cache_d_kv, page_table,
                                              state_d_win, d_phi_pe, d_phi_w1, d_phi_w2)
    c_p, c_s = groups(c_rows)
    d_p, d_s = groups(d_rows)
    win_p, win_s = groups(win_rows)
    w_p = win_p[:, -min(D_WINDOW, sp):]
    w_s = jnp.concatenate([state_d_win, win_s], axis=1)[:, -state_d_win.shape[1]:]
    x, _ = post(x, o1, w_out_odd, 1)
    y_p, y_s = groups(x)
    return (y_p, y_s, a_p, a_s, b_p, b_s, c_p, c_s, d_p, d_s, w_p, w_s)
```

```python
import functools

import jax
import jax.numpy as jnp
import numpy as np
from jax import lax
from jax.experimental import pallas as pl
from jax.experimental.pallas import tpu as pltpu

DEPTH = 2
PAGE_SIZE = 128
ROPE_THETA = 500000.0
LN_EPS = 1e-5
RMS_EPS = 1e-6
DN_ALPHA = (2 * DEPTH) ** 0.25

A_DIM = 128
A_HEADS = 16
A_BLOCK = 256
A_ROW = 2 * A_DIM
A_TOPK = 3
A_SCALE = A_DIM ** -0.5

B_HEADS = 16
B_NOPE = 128
B_ROPE = 64
B_VDIM = 128
B_QRANK = 768
B_KVRANK = 256
B_ROPE_THETA = 10000.0
B_SCALE = (B_NOPE + B_ROPE) ** -0.5

C_DIM = 128
C_HEADS = 16
C_IDX_HEADS = 32
C_IDX_DIM = 64
C_TOPK = 256
C_SCALE = C_DIM ** -0.5
C_IDX_W_SCALE = (C_IDX_HEADS * C_IDX_DIM) ** -0.5

D_DIM = 64
D_HEADS = 32
D_CMP_LEN = 32
D_CMP_STRIDE = 16
D_SLC_BLOCK = 64
D_SLC_TOPN = 16
D_SLC_LOCAL = 2
D_WINDOW = 512
D_SCALE = D_DIM ** -0.5

EVEN_SPLITS = (A_HEADS * A_DIM, A_DIM, A_DIM, B_QRANK, B_KVRANK, B_ROPE)
ODD_SPLITS = (C_HEADS * C_DIM, C_DIM, C_DIM, C_IDX_HEADS * C_IDX_DIM, C_IDX_DIM, C_IDX_HEADS,
              D_HEADS * D_DIM, 6 * D_DIM, 3 * D_HEADS)

LANE = 128
VMEM_LIMIT_BYTES = 56 * 1024 * 1024


def _cparams(*sem):
    return pltpu.CompilerParams(dimension_semantics=sem, vmem_limit_bytes=VMEM_LIMIT_BYTES)


def _pick_tile(n, cap, mult):
    best = None
    for t in range(mult, min(n, cap) + 1, mult):
        if n % t == 0:
            best = t
    assert best is not None, (n, cap, mult)
    return best


def _mm_body(x_ref, w_ref, o_ref, acc_ref, *, nk, act):
    k = pl.program_id(2)

    @pl.when(k == 0)
    def _():
        acc_ref[...] = jnp.zeros_like(acc_ref)

    acc_ref[...] += jnp.dot(x_ref[...].astype(jnp.bfloat16), w_ref[...],
                            preferred_element_type=jnp.float32)

    @pl.when(k == nk - 1)
    def _():
        r = acc_ref[...]
        if act == "relu2":
            r = jnp.maximum(r, 0.0)
            r = r * r
        o_ref[...] = r.astype(o_ref.dtype)


def matmul(x, w, *, act=None, out_dtype=jnp.float32):
    m, kdim = x.shape
    n = w.shape[1]
    tm = _pick_tile(m, 1024, 8)
    tn = _pick_tile(n, 2048, LANE)
    tk = _pick_tile(kdim, 512, LANE)
    nk = kdim // tk
    return pl.pallas_call(
        functools.partial(_mm_body, nk=nk, act=act),
        grid=(m // tm, n // tn, nk),
        in_specs=[pl.BlockSpec((tm, tk), lambda i, j, k: (i, k)),
                  pl.BlockSpec((tk, tn), lambda i, j, k: (k, j))],
        out_specs=pl.BlockSpec((tm, tn), lambda i, j, k: (i, j)),
        out_shape=jax.ShapeDtypeStruct((m, n), out_dtype),
        scratch_shapes=[pltpu.VMEM((tm, tn), jnp.float32)],
        compiler_params=_cparams("parallel", "parallel", "arbitrary"),
        name="matmul",
    )(x, w)


def _ln_body(x_ref, o_ref, g_ref, b_ref, y_ref, yb_ref):
    y = DN_ALPHA * x_ref[...] + o_ref[...]
    mu = jnp.mean(y, axis=-1, keepdims=True)
    yc = y - mu
    var = jnp.mean(yc * yc, axis=-1, keepdims=True)
    out = yc * lax.rsqrt(var + LN_EPS) * g_ref[...] + b_ref[...]
    y_ref[...] = out
    yb_ref[...] = out.astype(yb_ref.dtype)


def resid_layer_norm(x, o, g, b):
    m, d = x.shape
    tm = _pick_tile(m, 256, 8)
    row = pl.BlockSpec((tm, d), lambda i: (i, 0))
    vec = pl.BlockSpec((1, d), lambda i: (0, 0))
    return pl.pallas_call(
        _ln_body, grid=(m // tm,), in_specs=[row, row, vec, vec], out_specs=[row, row],
        out_shape=[jax.ShapeDtypeStruct((m, d), jnp.float32), jax.ShapeDtypeStruct((m, d), jnp.bfloat16)],
        compiler_params=_cparams("parallel"), name="resid_layer_norm",
    )(x, o, g.reshape(1, d), b.reshape(1, d))


NEG = -1e30
M_FLOOR = -1e29
BF = jnp.bfloat16
F32 = jnp.float32


def _dot(a, b):
    return jnp.dot(a, b, preferred_element_type=F32)


def _dot_nt(a, b):
    return lax.dot_general(a, b, (((1,), (1,)), ((), ())), preferred_element_type=F32)


def _rope(x, c, sm, sp, half):
    return x * c + pltpu.roll(x, LANE - half, 1) * sm + pltpu.roll(x, half, 1) * sp


def _add_head_mask(s, maskf, nh):
    tq, tk = maskf.shape
    return (s.reshape(nh, tq, tk) + maskf[None]).reshape(nh * tq, tk)


def _softmax_init(m_ref, l_ref, acc_ref):
    m_ref[...] = jnp.full(m_ref.shape, M_FLOOR, F32)
    l_ref[...] = jnp.zeros(l_ref.shape, F32)
    acc_ref[...] = jnp.zeros(acc_ref.shape, F32)


def _lane_partial_sum(p):
    tk = p.shape[1]
    if tk % LANE == 0:
        part = p[:, 0:LANE]
        for c in range(1, tk // LANE):
            part = part + p[:, c * LANE:(c + 1) * LANE]
        return part
    lane = lax.broadcasted_iota(jnp.int32, (p.shape[0], LANE), 1)
    return jnp.where(lane == 0, jnp.sum(p, axis=-1, keepdims=True), 0.0)


def _softmax_step(s, v, m_ref, l_ref, acc_ref, v_is_transposed=False):
    m_prev = m_ref[...]
    m_new = jnp.maximum(m_prev, jnp.max(s, axis=-1, keepdims=True))
    alpha = jnp.exp(m_prev - m_new)
    p = jnp.exp(s - m_new)
    l_ref[...] = alpha * l_ref[...] + _lane_partial_sum(p)
    pv = _dot_nt(p.astype(BF), v) if v_is_transposed else _dot(p.astype(BF), v)
    acc_ref[...] = alpha * acc_ref[...] + pv
    m_ref[...] = m_new


def _attend_tile(q_refs, score_fn, v, m_ref, l_ref, acc_ref, *, tq, heads_per_chunk, maskf=None, rowmask=None):
    ch = heads_per_chunk * tq
    n_chunks = q_refs[0].shape[0] // ch

    for i in range(n_chunks):
        rows = slice(i * ch, (i + 1) * ch)
        s = score_fn(*[r[rows, :] for r in q_refs])
        if maskf is not None:
            s = _add_head_mask(s, maskf, heads_per_chunk)
        if rowmask is not None:
            s = s + rowmask[rows]
        m_prev = m_ref[rows, :]
        m_new = jnp.maximum(m_prev, jnp.max(s, axis=-1, keepdims=True))
        alpha = jnp.exp(m_prev - m_new)
        p = jnp.exp(s - m_new)
        l_ref[rows, :] = alpha * l_ref[rows, :] + _lane_partial_sum(p)
        acc_ref[rows, :] = alpha * acc_ref[rows, :] + _dot(p.astype(BF), v)
        m_ref[rows, :] = m_new


def _softmax_finish(l_ref, acc_ref):
    l = jnp.sum(l_ref[...], axis=-1, keepdims=True)
    return acc_ref[...] / jnp.where(l > 0.0, l, 1.0)


def _top_lanes(g, avail, k):
    lane = lax.broadcasted_iota(jnp.int32, g.shape, 1)
    sel = jnp.zeros(g.shape, jnp.bool_)
    for _ in range(k):
        gm = jnp.where(avail, g, -jnp.inf)
        mx = jnp.max(gm, axis=-1, keepdims=True)
        idx = jnp.min(jnp.where(avail & (gm == mx), lane, jnp.int32(2 ** 30)), axis=-1, keepdims=True)
        pick = lane == idx
        sel = sel | pick
        avail = avail & jnp.logical_not(pick)
    return sel


def _rope_tables(pos, rot_dim, theta, width):
    half = rot_dim // 2
    inv = theta ** (-jnp.arange(0, rot_dim, 2, dtype=jnp.float32) / rot_dim)
    ang = pos.astype(jnp.float32)[:, None] * inv[None, :]
    c, s = jnp.cos(ang), jnp.sin(ang)
    t = pos.shape[0]
    one = jnp.ones((t, width - rot_dim), F32)
    z_half = jnp.zeros((t, half), F32)
    z_rest = jnp.zeros((t, width - rot_dim), F32)
    rep = LANE // width
    cc = jnp.tile(jnp.concatenate([c, c, one], axis=1), (1, rep))
    sm = jnp.tile(jnp.concatenate([-s, z_half, z_rest], axis=1), (1, rep))
    sp = jnp.tile(jnp.concatenate([z_half, s, z_rest], axis=1), (1, rep))
    return cc, sm, sp


def _tok_spec(tm, w):
    return pl.BlockSpec((tm, w), lambda i: (i, 0))


def _full_spec(shape):
    nd = len(shape)
    return pl.BlockSpec(shape, lambda *_: (0,) * nd)


_E_AK, _E_AV, _E_CQ, _E_CKV, _E_KR = 2048, 2176, 2304, 3072, 3328


def _prep_even_body(h_ref, ca, sma, spa, cb, smb, spb, gq_ref, gkv_ref,
                    aq_ref, arows_ref, cqn_ref, brows_ref):
    c, sm, sp = ca[...], sma[...], spa[...]
    ha = A_DIM // 8
    for h in range(A_HEADS):
        aq_ref[:, h * LANE:(h + 1) * LANE] = _rope(h_ref[:, h * LANE:(h + 1) * LANE], c, sm, sp, ha)
    arows_ref[:, 0:A_DIM] = _rope(h_ref[:, _E_AK:_E_AK + A_DIM], c, sm, sp, ha)
    arows_ref[:, A_DIM:2 * A_DIM] = h_ref[:, _E_AV:_E_AV + A_DIM]
    cq = h_ref[:, _E_CQ:_E_CQ + B_QRANK]
    cqn = cq * lax.rsqrt(jnp.mean(cq * cq, axis=-1, keepdims=True) + RMS_EPS) * gq_ref[...]
    cqn_ref[...] = cqn.astype(cqn_ref.dtype)
    ckv = h_ref[:, _E_CKV:_E_CKV + B_KVRANK]
    brows_ref[:, 0:B_KVRANK] = ckv * lax.rsqrt(jnp.mean(ckv * ckv, axis=-1, keepdims=True) + RMS_EPS) * gkv_ref[...]
    kr = _rope(h_ref[:, _E_KR:_E_KR + LANE], cb[...], smb[...], spb[...], B_ROPE // 2)
    brows_ref[:, B_KVRANK:B_KVRANK + B_ROPE] = kr[:, 0:B_ROPE]


def prep_even(h, tab_a, tab_b, g_cq, g_ckv):
    t, w = h.shape
    tm = _pick_tile(t, 256, 8)
    tab = _tok_spec(tm, LANE)
    return pl.pallas_call(
        _prep_even_body, grid=(t // tm,),
        in_specs=[_tok_spec(tm, w), tab, tab, tab, tab, tab, tab,
                  _full_spec((1, B_QRANK)), _full_spec((1, B_KVRANK))],
        out_specs=[_tok_spec(tm, A_HEADS * A_DIM), _tok_spec(tm, A_ROW),
                   _tok_spec(tm, B_QRANK), _tok_spec(tm, B_KVRANK + B_ROPE)],
        out_shape=[jax.ShapeDtypeStruct((t, A_HEADS * A_DIM), F32),
                   jax.ShapeDtypeStruct((t, A_ROW), F32),
                   jax.ShapeDtypeStruct((t, B_QRANK), BF),
                   jax.ShapeDtypeStruct((t, B_KVRANK + B_ROPE), F32)],
        compiler_params=_cparams("parallel"), name="prep_even",
    )(h, *tab_a, *tab_b, g_cq.reshape(1, -1), g_ckv.reshape(1, -1))


def _mla_qprep_body(q_ref, wuk_ref, cb, smb, spb, ql_ref, qr_ref):
    for h in range(B_HEADS):
        qn = q_ref[:, h * B_NOPE:(h + 1) * B_NOPE].astype(BF)
        ql_ref[:, h * B_KVRANK:(h + 1) * B_KVRANK] = _dot(qn, wuk_ref[h])
    c, sm, sp = cb[...], smb[...], spb[...]
    lane = lax.broadcasted_iota(jnp.int32, c.shape, 1)
    base = B_HEADS * B_NOPE
    for j in range(B_HEADS // 2):
        r = _rope(q_ref[:, base + j * LANE:base + (j + 1) * LANE], c, sm, sp, B_ROPE // 2)
        qr_ref[:, (2 * j) * LANE:(2 * j + 1) * LANE] = jnp.where(lane < B_ROPE, r, 0.0)
        qr_ref[:, (2 * j + 1) * LANE:(2 * j + 2) * LANE] = jnp.where(lane < B_ROPE, pltpu.roll(r, B_ROPE, 1), 0.0)


def mla_qprep(q, wuk_t, tab_b):
    t, w = q.shape
    tm = _pick_tile(t, 256, 8)
    tab = _tok_spec(tm, LANE)
    return pl.pallas_call(
        _mla_qprep_body, grid=(t // tm,),
        in_specs=[_tok_spec(tm, w), _full_spec(wuk_t.shape), tab, tab, tab],
        out_specs=[_tok_spec(tm, B_HEADS * B_KVRANK), _tok_spec(tm, B_HEADS * LANE)],
        out_shape=[jax.ShapeDtypeStruct((t, B_HEADS * B_KVRANK), F32),
                   jax.ShapeDtypeStruct((t, B_HEADS * LANE), F32)],
        compiler_params=_cparams("parallel"), name="mla_qprep",
    )(q, wuk_t, *tab_b)


def _mla_oproj_body(ol_ref, wuv_ref, o_ref):
    for h in range(B_HEADS):
        ol = ol_ref[:, h * B_KVRANK:(h + 1) * B_KVRANK].astype(BF)
        o_ref[:, h * B_VDIM:(h + 1) * B_VDIM] = _dot(ol, wuv_ref[h]).astype(o_ref.dtype)


def mla_oproj(o_lat, wuv):
    t, w = o_lat.shape
    tm = _pick_tile(t, 256, 8)
    return pl.pallas_call(
        _mla_oproj_body, grid=(t // tm,),
        in_specs=[_tok_spec(tm, w), _full_spec(wuv.shape)],
        out_specs=_tok_spec(tm, B_HEADS * B_VDIM),
        out_shape=jax.ShapeDtypeStruct((t, B_HEADS * B_VDIM), BF),
        compiler_params=_cparams("parallel"), name="mla_oproj",
    )(o_lat, wuv)


def _pages_per_step(n_pages, cap=16):
    return _pick_tile(n_pages, cap, 1)


def _page_specs(g_pages, width, col_block):
    def spec(g):
        return pl.BlockSpec((None, PAGE_SIZE, width),
                            lambda b, s, pt: (pt[b, s * g_pages + g], 0, col_block))
    return [spec(g) for g in range(g_pages)]


def _cat_pages(pages, lo, hi):
    return jnp.concatenate([p[:, lo:hi] for p in pages], axis=0)


def _page_specs_t(g_pages, feats, feat_block):
    def spec(g):
        return pl.BlockSpec((None, feats, PAGE_SIZE),
                            lambda b, s, pt: (pt[b, s * g_pages + g], feat_block, 0))
    return [spec(g) for g in range(g_pages)]


def _cat_pages_t(pages, lo, hi):
    return jnp.concatenate([p[lo:hi, :] for p in pages], axis=1)


_MOBA_HEADS_PER_CHUNK = 4


def _moba_p_body(aq_ref, rows_ref, o_ref, qs, m_ref, l_ref, acc_ref, *, tq, sp):
    nh = A_HEADS
    qi = pl.program_id(1)
    q0 = qi * tq
    own = q0 // A_BLOCK
    nblk = sp // A_BLOCK
    for h in range(nh):
        qs[h * tq:(h + 1) * tq, :] = aq_ref[:, h * A_DIM:(h + 1) * A_DIM].astype(BF)
    q = qs[...]
    km = [jnp.mean(rows_ref[n * A_BLOCK:(n + 1) * A_BLOCK, 0:A_DIM], axis=0, keepdims=True) for n in range(nblk)]
    km = jnp.concatenate(km + [jnp.zeros((LANE - nblk, A_DIM), F32)], axis=0)
    gate = _dot_nt(q, km.astype(BF))
    lane = lax.broadcasted_iota(jnp.int32, gate.shape, 1)
    selm = jnp.where(_top_lanes(gate, lane < own, A_TOPK), 0.0, NEG)
    _softmax_init(m_ref, l_ref, acc_ref)
    for n in range(nblk):
        @pl.when(n <= own)
        def _():
            k = rows_ref[n * A_BLOCK:(n + 1) * A_BLOCK, 0:A_DIM].astype(BF)
            v = rows_ref[n * A_BLOCK:(n + 1) * A_BLOCK, A_DIM:2 * A_DIM].astype(BF)
            qpos = q0 + lax.broadcasted_iota(jnp.int32, (tq, A_BLOCK), 0)
            kpos = n * A_BLOCK + lax.broadcasted_iota(jnp.int32, (tq, A_BLOCK), 1)
            is_own = n == own
            _attend_tile([qs], lambda qc: _dot_nt(qc, k) * A_SCALE, v, m_ref, l_ref, acc_ref,
                         tq=tq, heads_per_chunk=_MOBA_HEADS_PER_CHUNK,
                         maskf=jnp.where(is_own & (kpos > qpos), NEG, 0.0),
                         rowmask=jnp.where(is_own, 0.0, selm[:, n:n + 1]))
    res = _softmax_finish(l_ref, acc_ref)
    for h in range(nh):
        o_ref[:, h * A_DIM:(h + 1) * A_DIM] = res[h * tq:(h + 1) * tq, :].astype(o_ref.dtype)


def moba_prompt(aq, a_rows, nb, sp):
    tq = _pick_tile(sp, 128, 8)
    nq = sp // tq
    m = A_HEADS * tq
    return pl.pallas_call(
        functools.partial(_moba_p_body, tq=tq, sp=sp), grid=(nb, nq),
        in_specs=[pl.BlockSpec((tq, A_HEADS * A_DIM), lambda b, i: (b * nq + i, 0)),
                  pl.BlockSpec((sp, A_ROW), lambda b, i: (b, 0))],
        out_specs=pl.BlockSpec((tq, A_HEADS * A_DIM), lambda b, i: (b * nq + i, 0)),
        out_shape=jax.ShapeDtypeStruct((nb * sp, A_HEADS * A_DIM), BF),
        scratch_shapes=[pltpu.VMEM((m, A_DIM), BF), pltpu.VMEM((m, 1), F32), pltpu.VMEM((m, LANE), F32),
                        pltpu.VMEM((m, A_DIM), F32)],
        compiler_params=_cparams("parallel", "arbitrary"), name="moba_prompt",
    )(aq, a_rows)


def _moba_s_body(pt_ref, aq_ref, new_ref, *rest, g_pages, steps, ss):
    pages = rest[:g_pages]
    o_ref, qs, gs, ms, ls, accs = rest[g_pages:]
    nh = A_HEADS
    m_rows = nh * ss
    step = pl.program_id(1)
    bps = g_pages * PAGE_SIZE // A_BLOCK
    ppb = A_BLOCK // PAGE_SIZE

    @pl.when(step == 0)
    def _():
        for h in range(nh):
            qs[h * ss:(h + 1) * ss, :] = aq_ref[:, h * A_DIM:(h + 1) * A_DIM]
        gs[...] = jnp.full(gs.shape, NEG, F32)
        ms[...] = jnp.full(ms.shape, NEG, F32)
        ls[...] = jnp.zeros(ls.shape, F32)

    q = qs[...].astype(BF)
    lane = lax.broadcasted_iota(jnp.int32, (m_rows, LANE), 1)
    for i in range(bps):
        n = step * bps + i
        blk = pages[i * ppb:(i + 1) * ppb]
        k = _cat_pages(blk, 0, A_DIM).astype(BF)
        v = _cat_pages(blk, A_DIM, 2 * A_DIM).astype(BF)
        s = _dot_nt(q, k) * A_SCALE
        m_n = jnp.max(s, axis=-1, keepdims=True)
        p = jnp.exp(s - m_n)
        here = lane == n
        gs[...] = jnp.where(here, jnp.mean(s, axis=-1, keepdims=True), gs[...])
        ms[...] = jnp.where(here, m_n, ms[...])
        ls[...] = jnp.where(here, jnp.sum(p, axis=-1, keepdims=True), ls[...])
        accs[n] = _dot(p.astype(BF), v)

    @pl.when(step == steps - 1)
    def _():
        nblk = steps * bps
        sel = _top_lanes(gs[...], lane < nblk, A_TOPK)
        kn = new_ref[:, 0:A_DIM].astype(BF)
        vn = new_ref[:, A_DIM:2 * A_DIM].astype(BF)
        qrow = lax.broadcasted_iota(jnp.int32, (ss, ss), 0)
        kcol = lax.broadcasted_iota(jnp.int32, (ss, ss), 1)
        s_own = _add_head_mask(_dot_nt(q, kn) * A_SCALE, jnp.where(kcol <= qrow, 0.0, NEG), nh)
        m_own = jnp.max(s_own, axis=-1, keepdims=True)
        p_own = jnp.where(s_own > 0.5 * NEG, jnp.exp(s_own - m_own), 0.0)
        m_fin = jnp.maximum(jnp.max(jnp.where(sel, ms[...], NEG), axis=-1, keepdims=True), m_own)
        w = jnp.where(sel, jnp.exp(ms[...] - m_fin), 0.0)
        w_own = jnp.exp(m_own - m_fin)
        l_fin = jnp.sum(w * ls[...], axis=-1, keepdims=True) + w_own * jnp.sum(p_own, axis=-1, keepdims=True)
        acc = w_own * _dot(p_own.astype(BF), vn)
        for n in range(nblk):
            acc = acc + w[:, n:n + 1] * accs[n]
        res = acc / l_fin
        for h in range(nh):
            o_ref[:, h * A_DIM:(h + 1) * A_DIM] = res[h * ss:(h + 1) * ss, :].astype(o_ref.dtype)


def moba_sample(aq, a_rows, pool, page_table, tp, ns, ss):
    n_pages = page_table.shape[1]
    g_pages = _pages_per_step(n_pages)
    steps = n_pages // g_pages
    assert g_pages % (A_BLOCK // PAGE_SIZE) == 0 and n_pages * PAGE_SIZE // A_BLOCK <= LANE
    m = A_HEADS * ss
    tok = lambda w: pl.BlockSpec((ss, w), lambda b, s, pt: (tp // ss + b, 0))
    grid_spec = pltpu.PrefetchScalarGridSpec(
        num_scalar_prefetch=1, grid=(ns, steps),
        in_specs=[tok(A_HEADS * A_DIM), tok(A_ROW)] + _page_specs(g_pages, A_ROW, 0),
        out_specs=pl.BlockSpec((ss, A_HEADS * A_DIM), lambda b, s, pt: (b, 0)),
        scratch_shapes=[pltpu.VMEM((m, A_DIM), F32), pltpu.VMEM((m, LANE), F32), pltpu.VMEM((m, LANE), F32),
                        pltpu.VMEM((m, LANE), F32),
                        pltpu.VMEM((n_pages * PAGE_SIZE // A_BLOCK, m, A_DIM), F32)])
    return pl.pallas_call(
        functools.partial(_moba_s_body, g_pages=g_pages, steps=steps, ss=ss), grid_spec=grid_spec,
        out_shape=jax.ShapeDtypeStruct((ns * ss, A_HEADS * A_DIM), BF),
        compiler_params=_cparams("parallel", "arbitrary"), name="moba_sample",
    )(page_table, aq, a_rows, *([pool] * g_pages))


_MLA_HEADS_PER_CHUNK = 4
_DSA_HEADS_PER_CHUNK = 4


def _mla_scores(qls, qrs, blk):
    ckv = blk[:, 0:B_KVRANK].astype(BF)
    kr = blk[:, B_KVRANK:B_KVRANK + B_ROPE].astype(BF)
    return (_dot_nt(qls, ckv) + _dot_nt(qrs, kr)) * B_SCALE, ckv


def _mla_p_body(ql_ref, qr_ref, rows_ref, o_ref, qls, qrs, m_ref, l_ref, acc_ref, *, tq, tk):
    nh = B_HEADS
    qi = pl.program_id(1)
    q0 = qi * tq
    for h in range(nh):
        qls[h * tq:(h + 1) * tq, :] = ql_ref[:, h * B_KVRANK:(h + 1) * B_KVRANK].astype(BF)
        qrs[h * tq:(h + 1) * tq, :] = qr_ref[:, h * LANE:h * LANE + B_ROPE].astype(BF)
    _softmax_init(m_ref, l_ref, acc_ref)

    def body(j, carry):
        k0 = pl.multiple_of(j * tk, tk)
        blk = rows_ref[pl.ds(k0, tk), :]
        ckv = blk[:, 0:B_KVRANK].astype(BF)
        kr = blk[:, B_KVRANK:B_KVRANK + B_ROPE].astype(BF)
        qpos = q0 + lax.broadcasted_iota(jnp.int32, (tq, tk), 0)
        kpos = k0 + lax.broadcasted_iota(jnp.int32, (tq, tk), 1)
        _attend_tile([qls, qrs], lambda ql, qr: (_dot_nt(ql, ckv) + _dot_nt(qr, kr)) * B_SCALE, ckv,
                     m_ref, l_ref, acc_ref, tq=tq, heads_per_chunk=_MLA_HEADS_PER_CHUNK,
                     maskf=jnp.where(kpos <= qpos, 0.0, NEG))
        return carry

    lax.fori_loop(0, (q0 + tq + tk - 1) // tk, body, 0)
    res = _softmax_finish(l_ref, acc_ref)
    for h in range(nh):
        o_ref[:, h * B_KVRANK:(h + 1) * B_KVRANK] = res[h * tq:(h + 1) * tq, :]


def mla_prompt(q_lat, q_rope, b_rows, nb, sp):
    tq = _pick_tile(sp, 128, 8)
    tk = _pick_tile(sp, 256, 8)
    nq = sp // tq
    m = B_HEADS * tq
    tok = lambda w: pl.BlockSpec((tq, w), lambda b, i: (b * nq + i, 0))
    return pl.pallas_call(
        functools.partial(_mla_p_body, tq=tq, tk=tk), grid=(nb, nq),
        in_specs=[tok(B_HEADS * B_KVRANK), tok(B_HEADS * LANE),
                  pl.BlockSpec((sp, B_KVRANK + B_ROPE), lambda b, i: (b, 0))],
        out_specs=tok(B_HEADS * B_KVRANK),
        out_shape=jax.ShapeDtypeStruct((nb * sp, B_HEADS * B_KVRANK), F32),
        scratch_shapes=[pltpu.VMEM((m, B_KVRANK), BF), pltpu.VMEM((m, B_ROPE), BF),
                        pltpu.VMEM((m, 1), F32), pltpu.VMEM((m, LANE), F32), pltpu.VMEM((m, B_KVRANK), F32)],
        compiler_params=_cparams("parallel", "arbitrary"), name="mla_prompt",
    )(q_lat, q_rope, b_rows)


def _mla_s_body(pt_ref, ql_ref, qr_ref, new_ref, *rest, g_pages, steps, ss):
    pages = rest[:g_pages]
    o_ref, qls, qrs, m_ref, l_ref, acc_ref = rest[g_pages:]
    nh = B_HEADS
    step = pl.program_id(1)

    @pl.when(step == 0)
    def _():
        for h in range(nh):
            qls[h * ss:(h + 1) * ss, :] = ql_ref[:, h * B_KVRANK:(h + 1) * B_KVRANK]
            qrs[h * ss:(h + 1) * ss, :] = qr_ref[:, h * LANE:(h + 1) * LANE]
        _softmax_init(m_ref, l_ref, acc_ref)

    ql = qls[...].astype(BF)
    qr = qrs[:, 0:B_ROPE].astype(BF)
    ckv_t = _cat_pages_t(pages, 0, B_KVRANK).astype(BF)
    kr_t = _cat_pages_t(pages, B_KVRANK, B_KVRANK + B_ROPE).astype(BF)
    s = (_dot(ql, ckv_t) + _dot(qr, kr_t)) * B_SCALE
    _softmax_step(s, ckv_t, m_ref, l_ref, acc_ref, v_is_transposed=True)

    @pl.when(step == steps - 1)
    def _():
        s_new, ckv_new = _mla_scores(ql, qr, new_ref[...])
        qrow = lax.broadcasted_iota(jnp.int32, (ss, ss), 0)
        kcol = lax.broadcasted_iota(jnp.int32, (ss, ss), 1)
        _softmax_step(_add_head_mask(s_new, jnp.where(kcol <= qrow, 0.0, NEG), nh), ckv_new, m_ref, l_ref, acc_ref)
        res = _softmax_finish(l_ref, acc_ref)
        for h in range(nh):
            o_ref[:, h * B_KVRANK:(h + 1) * B_KVRANK] = res[h * ss:(h + 1) * ss, :]


def mla_sample(q_lat, q_rope, b_rows, pool, page_table, tp, ns, ss):
    n_pages = page_table.shape[1]
    g_pages = _pages_per_step(n_pages)
    steps = n_pages // g_pages
    m = B_HEADS * ss
    row_w = B_KVRANK + B_ROPE
    tok = lambda w: pl.BlockSpec((ss, w), lambda b, s, pt: (tp // ss + b, 0))
    grid_spec = pltpu.PrefetchScalarGridSpec(
        num_scalar_prefetch=1, grid=(ns, steps),
        in_specs=[tok(B_HEADS * B_KVRANK), tok(B_HEADS * LANE), tok(row_w)] + _page_specs_t(g_pages, row_w, 0),
        out_specs=pl.BlockSpec((ss, B_HEADS * B_KVRANK), lambda b, s, pt: (b, 0)),
        scratch_shapes=[pltpu.VMEM((m, B_KVRANK), F32), pltpu.VMEM((m, LANE), F32),
                        pltpu.VMEM((m, 1), F32), pltpu.VMEM((m, LANE), F32), pltpu.VMEM((m, B_KVRANK), F32)])
    return pl.pallas_call(
        functools.partial(_mla_s_body, g_pages=g_pages, steps=steps, ss=ss), grid_spec=grid_spec,
        out_shape=jax.ShapeDtypeStruct((ns * ss, B_HEADS * B_KVRANK), F32),
        compiler_params=_cparams("parallel", "arbitrary"), name="mla_sample",
    )(page_table, q_lat, q_rope, b_rows, *([jnp.swapaxes(pool, 1, 2)] * g_pages))


def even_mixers(h, pos, tp, nb, sp, ns, ss, pool_a, pool_b, page_table, g_cq, g_ckv, w_uq, w_ukv):
    tab_a = _rope_tables(pos, A_DIM // 4, ROPE_THETA, A_DIM)
    tab_b = _rope_tables(pos, B_ROPE, B_ROPE_THETA, B_ROPE)
    aq, a_rows, cqn, b_rows = prep_even(h, tab_a, tab_b, g_cq, g_ckv)
    w_uq_cols = jnp.concatenate([w_uq[:, :, :B_NOPE].reshape(B_QRANK, -1),
                                 w_uq[:, :, B_NOPE:].reshape(B_QRANK, -1)], axis=1).astype(BF)
    q = matmul(cqn, w_uq_cols)
    wuk_t = jnp.transpose(w_ukv[:, :, :B_NOPE], (1, 2, 0)).astype(BF)
    wuv = jnp.transpose(w_ukv[:, :, B_NOPE:], (1, 0, 2)).astype(BF)
    q_lat, q_rope = mla_qprep(q, wuk_t, tab_b)
    o_a = jnp.concatenate([moba_prompt(aq, a_rows, nb, sp),
                           moba_sample(aq, a_rows, pool_a, page_table, tp, ns, ss)], axis=0)
    o_lat = jnp.concatenate([mla_prompt(q_lat, q_rope, b_rows, nb, sp),
                             mla_sample(q_lat, q_rope, b_rows, pool_b, page_table, tp, ns, ss)], axis=0)
    o_b = mla_oproj(o_lat, wuv)
    return jnp.concatenate([o_a, o_b], axis=1), a_rows, b_rows


_O_CK, _O_CV, _O_IQ, _O_IK, _O_IW, _O_DQ, _O_DKV, _O_DG, _O_END = 2048, 2176, 2304, 4352, 4480, 4608, 6656, 7040, 7168
INT_MIN = -2 ** 31


def _odd_weight_cols(w):
    cuts = [int(c) for c in np.cumsum(ODD_SPLITS)[:-1]]
    cq, ck, cv, iq, ik, iw, dq, dkv, dg = jnp.split(w, cuts, axis=1)
    dg = dg.reshape(-1, D_HEADS, 3).transpose(0, 2, 1).reshape(-1, 3 * D_HEADS)
    pad = lambda a, n: jnp.pad(a, ((0, 0), (0, n - a.shape[1])))
    return jnp.concatenate([cq, ck, cv, iq, pad(ik, LANE), pad(iw, LANE), dq, dkv, pad(dg, LANE)], axis=1)


def _split_heads_64(r, lane):
    return jnp.where(lane < 64, r, 0.0), jnp.where(lane < 64, pltpu.roll(r, 64, 1), 0.0)


def _prep_odd_body(h_ref, ca, sma, spa, ci, smi, spi, cq_ref, crows_ref, iq_ref, iw_ref, dq_ref, drows_ref,
                   win_ref, gate_ref):
    c, sm, sp = ca[...], sma[...], spa[...]
    ha = C_DIM // 8
    for h in range(C_HEADS):
        cq_ref[:, h * LANE:(h + 1) * LANE] = _rope(h_ref[:, h * LANE:(h + 1) * LANE], c, sm, sp, ha)
    crows_ref[:, 0:C_DIM] = _rope(h_ref[:, _O_CK:_O_CK + C_DIM], c, sm, sp, ha)
    crows_ref[:, C_DIM:2 * C_DIM] = h_ref[:, _O_CV:_O_CV + C_DIM]
    c, sm, sp = ci[...], smi[...], spi[...]
    hi = C_IDX_DIM // 8
    lane = lax.broadcasted_iota(jnp.int32, c.shape, 1)
    ik = _rope(h_ref[:, _O_IK:_O_IK + LANE], c, sm, sp, hi)
    crows_ref[:, 2 * C_DIM:2 * C_DIM + C_IDX_DIM] = ik[:, 0:C_IDX_DIM]
    for j in range(C_IDX_HEADS // 2):
        a, b = _split_heads_64(_rope(h_ref[:, _O_IQ + j * LANE:_O_IQ + (j + 1) * LANE], c, sm, sp, hi), lane)
        iq_ref[:, (2 * j) * LANE:(2 * j + 1) * LANE] = a
        iq_ref[:, (2 * j + 1) * LANE:(2 * j + 2) * LANE] = b
    iw_ref[...] = h_ref[:, _O_IW:_O_IW + LANE] * C_IDX_W_SCALE
    for j in range(D_HEADS // 2):
        a, b = _split_heads_64(_rope(h_ref[:, _O_DQ + j * LANE:_O_DQ + (j + 1) * LANE], c, sm, sp, hi) * D_SCALE, lane)
        dq_ref[:, (2 * j) * LANE:(2 * j + 1) * LANE] = a
        dq_ref[:, (2 * j + 1) * LANE:(2 * j + 2) * LANE] = b
    left = lane < D_DIM
    c1, sm1, sp1 = jnp.where(left, c, 1.0), jnp.where(left, sm, 0.0), jnp.where(left, sp, 0.0)
    drows_ref[:, 0:LANE] = h_ref[:, _O_DKV:_O_DKV + LANE]
    drows_ref[:, LANE:2 * LANE] = _rope(h_ref[:, _O_DKV + LANE:_O_DKV + 2 * LANE], c1, sm1, sp1, hi)
    win_ref[...] = _rope(h_ref[:, _O_DKV + 2 * LANE:_O_DKV + 3 * LANE], c1, sm1, sp1, hi)
    gate_ref[...] = jax.nn.sigmoid(h_ref[:, _O_DG:_O_DG + LANE])


def prep_odd(h, tab_a, tab_i):
    t, w = h.shape
    tm = _pick_tile(t, 256, 8)
    tab = _tok_spec(tm, LANE)
    widths = [C_HEADS * C_DIM, 2 * C_DIM + C_IDX_DIM, C_IDX_HEADS * LANE, LANE, D_HEADS * LANE, 4 * D_DIM, LANE, LANE]
    return pl.pallas_call(
        _prep_odd_body, grid=(t // tm,),
        in_specs=[_tok_spec(tm, w), tab, tab, tab, tab, tab, tab],
        out_specs=[_tok_spec(tm, wd) for wd in widths],
        out_shape=[jax.ShapeDtypeStruct((t, wd), F32) for wd in widths],
        compiler_params=_cparams("parallel"), name="prep_odd",
    )(h, *tab_a, *tab_i)


def _sort_key(score, adm):
    b = lax.bitcast_convert_type(score + 0.0, jnp.int32)
    key = b ^ ((b >> 31) & jnp.int32(0x7FFFFFFF))
    return jnp.where(adm, key, jnp.int32(INT_MIN))


def _kth_largest(count_ge, shape, k):
    def body(it, ans):
        cand = ans + jnp.left_shift(jnp.int32(1), 31 - it)
        return jnp.where(count_ge(cand) >= k, cand, ans)
    return lax.fori_loop(0, 32, body, jnp.full(shape, INT_MIN, jnp.int32))


def _index_scores(iq_heads, iw, ikt):
    score = None
    for h in range(C_IDX_HEADS):
        rel = jnp.maximum(_dot_nt(iq_heads(h), ikt), 0.0)
        term = iw[:, h:h + 1] * rel
        score = term if score is None else score + term
    return score


def _tie_prefix(tie):
    r = lax.broadcasted_iota(jnp.int32, (LANE, LANE), 0)
    c = lax.broadcasted_iota(jnp.int32, (LANE, LANE), 1)
    upper = jnp.where(r < c, 1.0, 0.0).astype(BF)
    return _dot(jnp.where(tie, 1.0, 0.0).astype(BF), upper)


def _dsa_p_body(cq_ref, iq_ref, iw_ref, rows_ref, o_ref, qs, sck, m_ref, l_ref, acc_ref, *, tq, tk, sp, kk):
    nh = C_HEADS
    qi = pl.program_id(1)
    q0 = qi * tq
    ntile = sp // tk
    for h in range(nh):
        qs[h * tq:(h + 1) * tq, :] = cq_ref[:, h * C_DIM:(h + 1) * C_DIM].astype(BF)
    iw = iw_ref[...]
    qpos = q0 + lax.broadcasted_iota(jnp.int32, (tq, tk), 0)
    lane_k = lax.broadcasted_iota(jnp.int32, (tq, tk), 1)
    for j in range(ntile):
        @pl.when(j * tk < q0 + tq)
        def _():
            ikt = rows_ref[j * tk:(j + 1) * tk, 2 * C_DIM:2 * C_DIM + C_IDX_DIM].astype(BF)
            score = _index_scores(lambda h: iq_ref[:, h * LANE:h * LANE + C_IDX_DIM].astype(BF), iw, ikt)
            sck[:, j * tk:(j + 1) * tk] = _sort_key(score, j * tk + lane_k <= qpos)

        @pl.when(j * tk >= q0 + tq)
        def _():
            sck[:, j * tk:(j + 1) * tk] = jnp.full((tq, tk), INT_MIN, jnp.int32)

    def count_ge(t):
        return jnp.sum(jnp.where(sck[...] >= t, 1.0, 0.0), axis=-1, keepdims=True)

    thr = _kth_largest(count_ge, (tq, 1), float(kk))
    n_gt = jnp.sum(jnp.where(sck[...] > thr, 1.0, 0.0), axis=-1, keepdims=True)
    need = float(kk) - n_gt
    excess = (count_ge(thr) - n_gt > need) & (thr > INT_MIN)

    @pl.when(jnp.max(jnp.where(excess, 1.0, 0.0)) > 0.0)
    def _():
        carry = jnp.zeros((tq, 1), F32)
        for c in range(sp // LANE):
            keys = sck[:, c * LANE:(c + 1) * LANE]
            tie = keys == thr
            drop = excess & tie & (carry + _tie_prefix(tie) >= need)
            sck[:, c * LANE:(c + 1) * LANE] = jnp.where(drop, jnp.int32(INT_MIN), keys)
            carry = carry + jnp.sum(jnp.where(tie, 1.0, 0.0), axis=-1, keepdims=True)

    thr = jnp.maximum(thr, INT_MIN + 1)
    _softmax_init(m_ref, l_ref, acc_ref)
    for j in range(ntile):
        @pl.when(j * tk < q0 + tq)
        def _():
            k = rows_ref[j * tk:(j + 1) * tk, 0:C_DIM].astype(BF)
            v = rows_ref[j * tk:(j + 1) * tk, C_DIM:2 * C_DIM].astype(BF)
            maskf = jnp.where(sck[:, j * tk:(j + 1) * tk] >= thr, 0.0, NEG)
            _attend_tile([qs], lambda qc: _dot_nt(qc, k) * C_SCALE, v, m_ref, l_ref, acc_ref,
                         tq=tq, heads_per_chunk=_DSA_HEADS_PER_CHUNK, maskf=maskf)
    res = _softmax_finish(l_ref, acc_ref)
    for h in range(nh):
        o_ref[:, h * C_DIM:(h + 1) * C_DIM] = res[h * tq:(h + 1) * tq, :].astype(o_ref.dtype)


def dsa_prompt(cq, iq, iw, c_rows, nb, sp):
    tq = _pick_tile(sp, 128, 8)
    tk = _pick_tile(sp, 512, LANE)
    nq = sp // tq
    m = C_HEADS * tq
    tok = lambda w: pl.BlockSpec((tq, w), lambda b, i: (b * nq + i, 0))
    return pl.pallas_call(
        functools.partial(_dsa_p_body, tq=tq, tk=tk, sp=sp, kk=min(C_TOPK, sp // 4)), grid=(nb, nq),
        in_specs=[tok(C_HEADS * C_DIM), tok(C_IDX_HEADS * LANE), tok(LANE),
                  pl.BlockSpec((sp, 2 * C_DIM + C_IDX_DIM), lambda b, i: (b, 0))],
        out_specs=tok(C_HEADS * C_DIM),
        out_shape=jax.ShapeDtypeStruct((nb * sp, C_HEADS * C_DIM), BF),
        scratch_shapes=[pltpu.VMEM((m, C_DIM), BF), pltpu.VMEM((tq, sp), jnp.int32),
                        pltpu.VMEM((m, 1), F32), pltpu.VMEM((m, LANE), F32), pltpu.VMEM((m, C_DIM), F32)],
        compiler_params=_cparams("parallel", "arbitrary"), name="dsa_prompt",
    )(cq, iq, iw, c_rows)


def _dsa_si_body(pt_ref, iq_ref, iw_ref, new_ref, *rest, g_pages, steps, ss, kk):
    pages = rest[:g_pages]
    mask_ref, iqs, sc = rest[g_pages:]
    step = pl.program_id(1)
    w = g_pages * PAGE_SIZE
    nhi = C_IDX_HEADS

    @pl.when(step == 0)
    def _():
        for h in range(nhi):
            iqs[h * ss:(h + 1) * ss, :] = iq_ref[:, h * LANE:(h + 1) * LANE]

    iw = iw_ref[...]
    iq_all = iqs[:, 0:C_IDX_DIM].astype(BF)

    def scores(rel):
        acc = iw[:, 0:1] * rel[0:ss]
        for h in range(1, nhi):
            acc = acc + iw[:, h:h + 1] * rel[h * ss:(h + 1) * ss]
        return acc

    ik_t = _cat_pages_t(pages, 0, C_IDX_DIM).astype(BF)
    sc[step] = _sort_key(scores(jnp.maximum(_dot(iq_all, ik_t), 0.0)), jnp.full((ss, w), True))

    @pl.when(step == steps - 1)
    def _():
        ikn = new_ref[:, 2 * C_DIM:2 * C_DIM + C_IDX_DIM].astype(BF)
        s_new = scores(jnp.maximum(_dot_nt(iq_all, ikn), 0.0))
        qrow = lax.broadcasted_iota(jnp.int32, (ss, ss), 0)
        kcol = lax.broadcasted_iota(jnp.int32, (ss, ss), 1)
        sc[steps] = jnp.full((ss, w), INT_MIN, jnp.int32)
        sc[steps, :, 0:ss] = _sort_key(s_new, kcol <= qrow)

        def count_ge(t):
            return jnp.sum(jnp.sum(jnp.where(sc[...] >= t[None], 1.0, 0.0), axis=0), axis=-1, keepdims=True)

        thr = _kth_largest(count_ge, (ss, 1), float(kk))
        n_gt = jnp.sum(jnp.sum(jnp.where(sc[...] > thr[None], 1.0, 0.0), axis=0), axis=-1, keepdims=True)
        need = float(kk) - n_gt
        excess = (count_ge(thr) - n_gt > need) & (thr > INT_MIN)

        @pl.when(jnp.max(jnp.where(excess, 1.0, 0.0)) > 0.0)
        def _():
            def slot(s, carry):
                for c in range(w // LANE):
                    keys = sc[s, :, c * LANE:(c + 1) * LANE]
                    tie = keys == thr
                    drop = excess & tie & (carry + _tie_prefix(tie) >= need)
                    sc[s, :, c * LANE:(c + 1) * LANE] = jnp.where(drop, jnp.int32(INT_MIN), keys)
                    carry = carry + jnp.sum(jnp.where(tie, 1.0, 0.0), axis=-1, keepdims=True)
                return carry
            lax.fori_loop(0, steps + 1, slot, jnp.zeros((ss, 1), F32))

        thr2 = jnp.maximum(thr, INT_MIN + 1)
        mask_ref[...] = jnp.where(sc[...] >= thr2[None], 0.0, NEG)


def _dsa_sa_body(pt_ref, cq_ref, new_ref, mask_ref, mask_new_ref, *rest, g_pages, steps, ss):
    pages = rest[:g_pages]
    o_ref, qs, m_ref, l_ref, acc_ref = rest[g_pages:]
    nh = C_HEADS
    step = pl.program_id(1)

    @pl.when(step == 0)
    def _():
        for h in range(nh):
            qs[h * ss:(h + 1) * ss, :] = cq_ref[:, h * C_DIM:(h + 1) * C_DIM]
        _softmax_init(m_ref, l_ref, acc_ref)

    q = qs[...].astype(BF)
    k_t = _cat_pages_t(pages, 0, C_DIM).astype(BF)
    v_t = _cat_pages_t(pages, C_DIM, 2 * C_DIM).astype(BF)
    _softmax_step(_add_head_mask(_dot(q, k_t) * C_SCALE, mask_ref[...], nh), v_t, m_ref, l_ref, acc_ref,
                  v_is_transposed=True)

    @pl.when(step == steps - 1)
    def _():
        kn = new_ref[:, 0:C_DIM].astype(BF)
        vn = new_ref[:, C_DIM:2 * C_DIM].astype(BF)
        s_new = _add_head_mask(_dot_nt(q, kn) * C_SCALE, mask_new_ref[:, 0:ss], nh)
        _softmax_step(s_new, vn, m_ref, l_ref, acc_ref)
        res = _softmax_finish(l_ref, acc_ref)
        for h in range(nh):
            o_ref[:, h * C_DIM:(h + 1) * C_DIM] = res[h * ss:(h + 1) * ss, :].astype(o_ref.dtype)


def dsa_sample(cq, iq, iw, c_rows, pool, page_table, tp, ns, ss):
    n_pages = page_table.shape[1]
    g_pages = _pages_per_step(n_pages)
    steps = n_pages // g_pages
    w = g_pages * PAGE_SIZE
    row_w = 2 * C_DIM + C_IDX_DIM
    kk = min(C_TOPK, (n_pages * PAGE_SIZE + ss) // 4)
    tok = lambda wd: pl.BlockSpec((ss, wd), lambda b, s, pt: (tp // ss + b, 0))
    pool_t = jnp.swapaxes(pool, 1, 2)
    mask = pl.pallas_call(
        functools.partial(_dsa_si_body, g_pages=g_pages, steps=steps, ss=ss, kk=kk),
        grid_spec=pltpu.PrefetchScalarGridSpec(
            num_scalar_prefetch=1, grid=(ns, steps),
            in_specs=[tok(C_IDX_HEADS * LANE), tok(LANE), tok(row_w)]
            + _page_specs_t(g_pages, C_IDX_DIM, 2 * C_DIM // C_IDX_DIM),
            out_specs=pl.BlockSpec((None, steps + 1, ss, w), lambda b, s, pt: (b, 0, 0, 0)),
            scratch_shapes=[pltpu.VMEM((C_IDX_HEADS * ss, LANE), F32), pltpu.VMEM((steps + 1, ss, w), jnp.int32)]),
        out_shape=jax.ShapeDtypeStruct((ns, steps + 1, ss, w), F32),
        compiler_params=_cparams("parallel", "arbitrary"), name="dsa_sample_index",
    )(page_table, iq, iw, c_rows, *([pool_t] * g_pages))
    m = C_HEADS * ss
    return pl.pallas_call(
        functools.partial(_dsa_sa_body, g_pages=g_pages, steps=steps, ss=ss),
        grid_spec=pltpu.PrefetchScalarGridSpec(
            num_scalar_prefetch=1, grid=(ns, steps),
            in_specs=[tok(C_HEADS * C_DIM), tok(row_w),
                      pl.BlockSpec((None, None, ss, w), lambda b, s, pt: (b, s, 0, 0)),
                      pl.BlockSpec((None, None, ss, w), lambda b, s, pt: (b, steps, 0, 0))]
            + _page_specs_t(g_pages, 2 * C_DIM, 0),
            out_specs=pl.BlockSpec((ss, C_HEADS * C_DIM), lambda b, s, pt: (b, 0)),
            scratch_shapes=[pltpu.VMEM((m, C_DIM), F32), pltpu.VMEM((m, 1), F32), pltpu.VMEM((m, LANE), F32),
                            pltpu.VMEM((m, C_DIM), F32)]),
        out_shape=jax.ShapeDtypeStruct((ns * ss, C_HEADS * C_DIM), BF),
        compiler_params=_cparams("parallel", "arbitrary"), name="dsa_sample_attend",
    )(page_table, cq, c_rows, mask, mask, *([pool_t] * g_pages))


_CH = D_CMP_STRIDE
_NSA_HEADS_PER_CHUNK = 4
_SLC_PER_CH = D_SLC_BLOCK // D_CMP_STRIDE


def _round_up(n, m):
    return -(-n // m) * m


def _phi_weights(pe, w1, w2):
    nch = D_CMP_LEN // _CH
    halves = []
    for half in range(nch):
        wk = w1[0, half * _CH * D_DIM:(half + 1) * _CH * D_DIM].reshape(_CH, D_DIM, -1)
        wv = w1[1, half * _CH * D_DIM:(half + 1) * _CH * D_DIM].reshape(_CH, D_DIM, -1)
        z = jnp.zeros_like(wk)
        blk = jnp.concatenate([jnp.concatenate([wk, z], axis=2), jnp.concatenate([z, wv], axis=2)], axis=1)
        halves.append(blk.reshape(_CH * 2 * D_DIM, -1).astype(BF))
    pes = [jnp.concatenate([pe[0, half * _CH:(half + 1) * _CH], pe[1, half * _CH:(half + 1) * _CH]],
                           axis=1).reshape(1, -1) for half in range(nch)]
    z2 = jnp.zeros_like(w2[0])
    w2b = jnp.concatenate([jnp.concatenate([w2[0], z2], axis=1), jnp.concatenate([z2, w2[1]], axis=1)], axis=0)
    return jnp.concatenate(halves, axis=1), pes[0], pes[1], w2b.astype(BF)


def _chunk_rows(read, n_chunks):
    return jnp.concatenate([read(j, n_chunks) for j in range(_CH)], axis=1)


def _compress_pre(x, wcat, pelo, pehi):
    y = _dot(x.astype(BF), wcat)
    pe = jnp.concatenate([jnp.broadcast_to(pelo, (4, pelo.shape[1])), jnp.broadcast_to(pehi, (4, pehi.shape[1]))], axis=0)
    bias = _dot(pe.astype(BF), wcat)
    lo = y[:, 0:LANE] + bias[0:1, 0:LANE]
    hi = y[:, LANE:2 * LANE] + bias[4:5, LANE:2 * LANE]
    return lo, hi


def _compress_post(lo, hi, w2b):
    n = lo.shape[0]
    return _dot(jax.nn.silu(lo + pltpu.roll(hi, n - 1, 0)).astype(BF), w2b)


def _cmp_branch(q, kvc, qpos, n_tok, nh, tq):
    ncp = kvc.shape[0]
    c_idx = lax.broadcasted_iota(jnp.int32, (tq, ncp), 1)
    vis = (c_idx < n_tok) & (c_idx * D_CMP_STRIDE + (D_CMP_LEN - 1) <= qpos)
    maskf = jnp.where(vis, 0.0, NEG)
    hpc = min(nh, max(1, 512 // tq))
    outs, psum = [], None
    for c0 in range(0, nh, hpc):
        s = _add_head_mask(_dot_nt(q[c0 * tq:(c0 + hpc) * tq], kvc), maskf, hpc)
        m = jnp.maximum(jnp.max(s, axis=-1, keepdims=True), M_FLOOR)
        p = jnp.exp(s - m)
        l = jnp.sum(p, axis=-1, keepdims=True)
        l = jnp.where(l > 0.0, l, 1.0)
        outs.append(_dot(p.astype(BF), kvc) / l)
        pn = p / l
        for h in range(hpc):
            psum = pn[h * tq:(h + 1) * tq] if psum is None else psum + pn[h * tq:(h + 1) * tq]
    return jnp.concatenate(outs, axis=0), psum


def _block_importance(psum, nsbp):
    ncp = psum.shape[1]
    c = lax.broadcasted_iota(jnp.int32, (ncp, nsbp), 0)
    j = lax.broadcasted_iota(jnp.int32, (ncp, nsbp), 1)
    rc = D_CMP_LEN // D_CMP_STRIDE
    band = jnp.where((c >= _SLC_PER_CH * j - (rc - 1)) & (c <= _SLC_PER_CH * j + _SLC_PER_CH - 1), 1.0, 0.0).astype(BF)
    hi = psum.astype(BF)
    r1 = psum - hi.astype(F32)
    mid = r1.astype(BF)
    lo = (r1 - mid.astype(F32)).astype(BF)
    return _dot(hi, band) + _dot(mid, band) + _dot(lo, band)


def _select_blocks(imp, qpos_col, nsb):
    jb = lax.broadcasted_iota(jnp.int32, imp.shape, 1)
    bt = qpos_col // D_SLC_BLOCK
    forced = (jb == 0) | (jb >= bt - (D_SLC_LOCAL - 1))
    val = jnp.where(forced, jnp.inf, imp)
    sel = _top_lanes(val, (jb <= bt) & (jb < nsb), min(D_SLC_TOPN, nsb))
    return jnp.where(sel, 1.0, 0.0)


def _slc_mask(selm, k0, tk, qpos):
    nsbp = selm.shape[1]
    jb = lax.broadcasted_iota(jnp.int32, (nsbp, tk), 0)
    kk = lax.broadcasted_iota(jnp.int32, (nsbp, tk), 1)
    expand = jnp.where((k0 + kk) // D_SLC_BLOCK == jb, 1.0, 0.0).astype(BF)
    chosen = _dot(selm.astype(BF), expand) > 0.5
    kpos = k0 + lax.broadcasted_iota(jnp.int32, (selm.shape[0], tk), 1)
    return jnp.where(chosen & (kpos <= qpos), 0.0, NEG)


def _nsa_combine(o_cmp, o_slc, o_win, gates, nh, tq, o_ref):
    lane = lax.broadcasted_iota(jnp.int32, (tq, LANE), 1)
    outs = []
    for h in range(nh):
        r = slice(h * tq, (h + 1) * tq)
        outs.append(gates[:, h:h + 1] * o_cmp[r] + gates[:, nh + h:nh + h + 1] * o_slc[r]
                    + gates[:, 2 * nh + h:2 * nh + h + 1] * o_win[r])
    for j in range(nh // 2):
        o_ref[:, j * LANE:(j + 1) * LANE] = jnp.where(lane < D_DIM, pltpu.roll(outs[2 * j], D_DIM, 1),
                                                      outs[2 * j + 1]).astype(o_ref.dtype)


def _nsa_p_body(dq_ref, gate_ref, cmp_ref, slc_ref, win_ref, wcat_ref, pelo_ref, pehi_ref, w2_ref, o_ref,
                qs, kvc, m_ref, l_ref, acc_ref, m2_ref, l2_ref, acc2_ref, *, tq, tk, sp):
    nh = D_HEADS
    qi = pl.program_id(1)
    q0 = qi * tq
    n_ch = sp // _CH
    ncp = kvc.shape[0]
    nsb = sp // D_SLC_BLOCK
    nsbp = _round_up(nsb, LANE)

    @pl.when(qi == 0)
    def _():
        x = _chunk_rows(lambda j, n: cmp_ref[pl.ds(j, n, stride=_CH), :], n_ch)
        lo, hi = _compress_pre(x, wcat_ref[...], pelo_ref[...], pehi_ref[...])
        kvc[...] = jnp.zeros(kvc.shape, kvc.dtype)
        kvc[0:n_ch, :] = _compress_post(lo, hi, w2_ref[...]).astype(kvc.dtype)

    for h in range(nh):
        qs[h * tq:(h + 1) * tq, :] = dq_ref[:, h * LANE:(h + 1) * LANE].astype(BF)
    q = qs[...]
    qpos_col = q0 + lax.broadcasted_iota(jnp.int32, (tq, 1), 0)
    o_cmp, psum = _cmp_branch(q, kvc[...], qpos_col, n_ch - 1, nh, tq)
    selm = _select_blocks(_block_importance(psum, nsbp), qpos_col, nsb)
    _softmax_init(m_ref, l_ref, acc_ref)
    for j in range(sp // tk):
        @pl.when(j * tk < q0 + tq)
        def _():
            kv = slc_ref[j * tk:(j + 1) * tk, :].astype(BF)
            maskf = _slc_mask(selm, j * tk, tk, qpos_col)
            _attend_tile([qs], lambda qc: _dot_nt(qc, kv), kv, m_ref, l_ref, acc_ref,
                         tq=tq, heads_per_chunk=_NSA_HEADS_PER_CHUNK, maskf=maskf)
    o_slc = _softmax_finish(l_ref, acc_ref)
    _softmax_init(m2_ref, l2_ref, acc2_ref)
    base = (q0 // tk) * tk
    for w in range(D_WINDOW // tk + 1):
        k0 = base - D_WINDOW + w * tk

        @pl.when(k0 >= 0)
        def _():
            kv = win_ref[pl.ds(pl.multiple_of(k0, tk), tk), :].astype(BF)
            dist = qpos_col - (k0 + lax.broadcasted_iota(jnp.int32, (tq, tk), 1))
            maskf = jnp.where((dist >= 0) & (dist < D_WINDOW), 0.0, NEG)
            _attend_tile([qs], lambda qc: _dot_nt(qc, kv), kv, m2_ref, l2_ref, acc2_ref,
                         tq=tq, heads_per_chunk=_NSA_HEADS_PER_CHUNK, maskf=maskf)
    o_win = _softmax_finish(l2_ref, acc2_ref)
    _nsa_combine(o_cmp, o_slc, o_win, gate_ref[...], nh, tq, o_ref)


def nsa_prompt(dq, gates, d_rows, win_rows, phi, nb, sp):
    tq = _pick_tile(sp, 128, 8)
    tk = _pick_tile(sp, 256, LANE)
    assert D_WINDOW % tk == 0 and tk % tq == 0
    nq = sp // tq
    m = D_HEADS * tq
    ncp = _round_up(sp // _CH, LANE)
    tok = lambda w: pl.BlockSpec((tq, w), lambda b, i: (b * nq + i, 0))
    seq = lambda col: pl.BlockSpec((sp, LANE), lambda b, i: (b, col))
    stats = [pltpu.VMEM((m, 1), F32), pltpu.VMEM((m, LANE), F32), pltpu.VMEM((m, LANE), F32)]
    return pl.pallas_call(
        functools.partial(_nsa_p_body, tq=tq, tk=tk, sp=sp), grid=(nb, nq),
        in_specs=[tok(D_HEADS * LANE), tok(LANE), seq(0), seq(1), seq(0)] + [_full_spec(a.shape) for a in phi],
        out_specs=tok(D_HEADS * D_DIM),
        out_shape=jax.ShapeDtypeStruct((nb * sp, D_HEADS * D_DIM), BF),
        scratch_shapes=[pltpu.VMEM((m, LANE), BF), pltpu.VMEM((ncp, LANE), BF)] + stats + stats,
        compiler_params=_cparams("parallel", "arbitrary"), name="nsa_prompt",
    )(dq, gates, d_rows, d_rows, win_rows, *phi)


def _nsa_sc_body(pt_ref, dq_ref, wcat_ref, pelo_ref, pehi_ref, w2_ref, *rest,
                 g_pages, steps, ss, past_len, nsb):
    pages = rest[:g_pages]
    ocmp_ref, selm_ref, qs, los, his = rest[g_pages:]
    nh = D_HEADS
    step = pl.program_id(1)
    cpp = PAGE_SIZE // _CH
    n_ch = past_len // _CH

    @pl.when(step == 0)
    def _():
        for h in range(nh):
            qs[h * ss:(h + 1) * ss, :] = dq_ref[:, h * LANE:(h + 1) * LANE]

    x = jnp.concatenate([_chunk_rows(lambda j, n: p[pl.ds(j, n, stride=_CH), :], cpp) for p in pages], axis=0)
    lo, hi = _compress_pre(x, wcat_ref[...], pelo_ref[...], pehi_ref[...])
    r0 = pl.multiple_of(step * (g_pages * cpp), g_pages * cpp)
    los[pl.ds(r0, g_pages * cpp), :] = lo
    his[pl.ds(r0, g_pages * cpp), :] = hi

    @pl.when(step == steps - 1)
    def _():
        kvc = _compress_post(los[...], his[...], w2_ref[...]).astype(BF)
        qpos_col = past_len + lax.broadcasted_iota(jnp.int32, (ss, 1), 0)
        o_cmp, psum = _cmp_branch(qs[...].astype(BF), kvc, qpos_col, n_ch - 1, nh, ss)
        ocmp_ref[...] = o_cmp
        selm_ref[...] = _select_blocks(_block_importance(psum, selm_ref.shape[-1]), qpos_col, nsb)


def _nsa_sa_body(pt_ref, dq_ref, gate_ref, new_ref, wnew_ref, wstate_ref, ocmp_ref, selm_ref, *rest,
                 g_pages, steps, ss, past_len):
    pages = rest[:g_pages]
    o_ref, qs, m_ref, l_ref, acc_ref = rest[g_pages:]
    nh = D_HEADS
    step = pl.program_id(1)
    w = g_pages * PAGE_SIZE

    @pl.when(step == 0)
    def _():
        for h in range(nh):
            qs[h * ss:(h + 1) * ss, :] = dq_ref[:, h * LANE:(h + 1) * LANE]
        _softmax_init(m_ref, l_ref, acc_ref)

    q = qs[...].astype(BF)
    qpos_col = past_len + lax.broadcasted_iota(jnp.int32, (ss, 1), 0)
    selm = selm_ref[...]
    kv = jnp.concatenate([p[...] for p in pages], axis=0).astype(BF)
    maskf = _slc_mask(selm, step * w, w, qpos_col)
    _softmax_step(_add_head_mask(_dot_nt(q, kv), maskf, nh), kv, m_ref, l_ref, acc_ref)

    @pl.when(step == steps - 1)
    def _():
        kvn = new_ref[:, LANE:2 * LANE].astype(BF)
        maskn = _slc_mask(selm, past_len, ss, qpos_col)
        _softmax_step(_add_head_mask(_dot_nt(q, kvn), maskn, nh), kvn, m_ref, l_ref, acc_ref)
        o_slc = _softmax_finish(l_ref, acc_ref)
        _softmax_init(m_ref, l_ref, acc_ref)
        wbuf = wstate_ref.shape[0]
        for kvw, k0 in ((wstate_ref[...].astype(BF), past_len - wbuf), (wnew_ref[...].astype(BF), past_len)):
            n = kvw.shape[0]
            dist = qpos_col - (k0 + lax.broadcasted_iota(jnp.int32, (ss, n), 1))
            maskw = jnp.where((dist >= 0) & (dist < D_WINDOW), 0.0, NEG)
            _softmax_step(_add_head_mask(_dot_nt(q, kvw), maskw, nh), kvw, m_ref, l_ref, acc_ref)
        o_win = _softmax_finish(l_ref, acc_ref)
        _nsa_combine(ocmp_ref[...], o_slc, o_win, gate_ref[...], nh, ss, o_ref)


def nsa_sample(dq, gates, d_rows, win_rows, win_state, phi, pool, page_table, tp, ns, ss):
    n_pages = page_table.shape[1]
    g_pages = _pages_per_step(n_pages)
    steps = n_pages // g_pages
    past_len = n_pages * PAGE_SIZE
    assert ss <= D_CMP_STRIDE and past_len % D_SLC_BLOCK == 0
    n_ch = past_len // _CH
    nsb = -(-(past_len + ss) // D_SLC_BLOCK)
    nsbp = _round_up(nsb, LANE)
    m = D_HEADS * ss
    tok = lambda wd: pl.BlockSpec((ss, wd), lambda b, s, pt: (tp // ss + b, 0))
    per_seq = lambda shape: pl.BlockSpec((None,) + shape, lambda b, s, pt: (b,) + (0,) * len(shape))
    phi_specs = [pl.BlockSpec(a.shape, lambda b, s, pt, nd=a.ndim: (0,) * nd) for a in phi]
    gc_pages = _pages_per_step(n_pages, 32)
    o_cmp, selm = pl.pallas_call(
        functools.partial(_nsa_sc_body, g_pages=gc_pages, steps=n_pages // gc_pages, ss=ss, past_len=past_len,
                          nsb=nsb),
        grid_spec=pltpu.PrefetchScalarGridSpec(
            num_scalar_prefetch=1, grid=(ns, n_pages // gc_pages),
            in_specs=[tok(D_HEADS * LANE)] + phi_specs + _page_specs(gc_pages, LANE, 0),
            out_specs=[per_seq((m, LANE)), per_seq((ss, nsbp))],
            scratch_shapes=[pltpu.VMEM((m, LANE), F32), pltpu.VMEM((n_ch, LANE), F32), pltpu.VMEM((n_ch, LANE), F32)]),
        out_shape=[jax.ShapeDtypeStruct((ns, m, LANE), F32), jax.ShapeDtypeStruct((ns, ss, nsbp), F32)],
        compiler_params=_cparams("parallel", "arbitrary"), name="nsa_sample_compress",
    )(page_table, dq, *phi, *([pool] * gc_pages))
    wbuf = win_state.shape[1]
    return pl.pallas_call(
        functools.partial(_nsa_sa_body, g_pages=g_pages, steps=steps, ss=ss, past_len=past_len),
        grid_spec=pltpu.PrefetchScalarGridSpec(
            num_scalar_prefetch=1, grid=(ns, steps),
            in_specs=[tok(D_HEADS * LANE), tok(LANE), tok(4 * D_DIM), tok(LANE), per_seq((wbuf, LANE)),
                      per_seq((m, LANE)), per_seq((ss, nsbp))] + _page_specs(g_pages, LANE, 1),
            out_specs=pl.BlockSpec((ss, D_HEADS * D_DIM), lambda b, s, pt: (b, 0)),
            scratch_shapes=[pltpu.VMEM((m, LANE), F32), pltpu.VMEM((m, 1), F32), pltpu.VMEM((m, LANE), F32),
                            pltpu.VMEM((m, LANE), F32)]),
        out_shape=jax.ShapeDtypeStruct((ns * ss, D_HEADS * D_DIM), BF),
        compiler_params=_cparams("parallel", "arbitrary"), name="nsa_sample_attend",
    )(page_table, dq, gates, d_rows, win_rows, win_state, o_cmp, selm, *([pool] * g_pages))


def odd_mixers(h, pos, tp, nb, sp, ns, ss, pool_c, pool_d, page_table, win_state, phi_pe, phi_w1, phi_w2):
    tab_a = _rope_tables(pos, C_DIM // 4, ROPE_THETA, C_DIM)
    tab_i = _rope_tables(pos, C_IDX_DIM // 4, ROPE_THETA, C_IDX_DIM)
    cq, c_rows, iq, iw, dq, d_rows, win_rows, gates = prep_odd(h, tab_a, tab_i)
    phi = _phi_weights(phi_pe, phi_w1, phi_w2)
    o_c = jnp.concatenate([dsa_prompt(cq, iq, iw, c_rows, nb, sp),
                           dsa_sample(cq, iq, iw, c_rows, pool_c, page_table, tp, ns, ss)], axis=0)
    o_d = jnp.concatenate([nsa_prompt(dq, gates, d_rows, win_rows, phi, nb, sp),
                           nsa_sample(dq, gates, d_rows, win_rows, win_state, phi, pool_d, page_table, tp, ns, ss)],
                          axis=0)
    return jnp.concatenate([o_c, o_d], axis=1), c_rows, d_rows, win_rows


def _pad_cols(w, mult):
    n = w.shape[1]
    npad = -(-n // mult) * mult
    return jnp.pad(w, ((0, 0), (0, npad - n))) if npad != n else w


def kernel(x_prompt, x_sample, cache_a_kv, cache_b_latent, cache_c_kvi, cache_d_kv, state_d_win, page_table,
           w_in_even, b_g_cq, b_g_ckv, b_w_uq, b_w_ukv, w_out_even,
           w_in_odd, d_phi_pe, d_phi_w1, d_phi_w2, w_out_odd,
           mlp_w1, mlp_w2, ln_g, ln_b):
    nb, sp, d = x_prompt.shape
    ns, ss, _ = x_sample.shape
    tp = nb * sp
    past_len = page_table.shape[1] * PAGE_SIZE
    pos_p = jnp.arange(sp, dtype=jnp.int32)
    pos_s = past_len + jnp.arange(ss, dtype=jnp.int32)
    bf = jnp.bfloat16
    x = jnp.concatenate([x_prompt.reshape(tp, d), x_sample.reshape(ns * ss, d)], axis=0)

    def groups(h):
        return h[:tp].reshape(nb, sp, -1), h[tp:].reshape(ns, ss, -1)

    def post(x, o, w_out, layer):
        x, xb = resid_layer_norm(x, matmul(o, w_out.astype(bf)), ln_g[layer, 0], ln_b[layer, 0])
        hm = matmul(xb, mlp_w1[layer].astype(bf), act="relu2", out_dtype=bf)
        return resid_layer_norm(x, matmul(hm, mlp_w2[layer].astype(bf)), ln_g[layer, 1], ln_b[layer, 1])

    pos_all = jnp.concatenate([jnp.tile(pos_p, nb), jnp.tile(pos_s, ns)])
    h0 = matmul(x, _pad_cols(w_in_even, LANE).astype(bf))
    o0, a_rows, b_rows = even_mixers(h0, pos_all, tp, nb, sp, ns, ss, cache_a_kv, cache_b_latent, page_table,
                                     b_g_cq, b_g_ckv, b_w_uq, b_w_ukv)
    a_p, a_s = groups(a_rows)
    b_p, b_s = groups(b_rows)
    x, xb = post(x, o0, w_out_even, 0)
    h1 = matmul(xb, _odd_weight_cols(w_in_odd).astype(bf))
    o1, c_rows, d_rows, win_rows = odd_mixers(h1, pos_all, tp, nb, sp, ns, ss, cache_c_kvi, cache_d_kv, page_table,
                                              state_d_win, d_phi_pe, d_phi_w1, d_phi_w2)
    c_p, c_s = groups(c_rows)
    d_p, d_s = groups(d_rows)
    win_p, win_s = groups(win_rows)
    w_p = win_p[:, -min(D_WINDOW, sp):]
    w_s = jnp.concatenate([state_d_win, win_s], axis=1)[:, -state_d_win.shape[1]:]
    x, _ = post(x, o1, w_out_odd, 1)
    y_p, y_s = groups(x)
    return (y_p, y_s, a_p, a_s, b_p, b_s, c_p, c_s, d_p, d_s, w_p, w_s)
```

```python
import functools

import jax
import jax.numpy as jnp
import numpy as np
from jax import lax
from jax.experimental import pallas as pl
from jax.experimental.pallas import tpu as pltpu

DEPTH = 2
PAGE_SIZE = 128
ROPE_THETA = 500000.0
LN_EPS = 1e-5
RMS_EPS = 1e-6
DN_ALPHA = (2 * DEPTH) ** 0.25

A_DIM = 128
A_HEADS = 16
A_BLOCK = 256
A_ROW = 2 * A_DIM
A_TOPK = 3
A_SCALE = A_DIM ** -0.5

B_HEADS = 16
B_NOPE = 128
B_ROPE = 64
B_VDIM = 128
B_QRANK = 768
B_KVRANK = 256
B_ROPE_THETA = 10000.0
B_SCALE = (B_NOPE + B_ROPE) ** -0.5

C_DIM = 128
C_HEADS = 16
C_IDX_HEADS = 32
C_IDX_DIM = 64
C_TOPK = 256
C_SCALE = C_DIM ** -0.5
C_IDX_W_SCALE = (C_IDX_HEADS * C_IDX_DIM) ** -0.5

D_DIM = 64
D_HEADS = 32
D_CMP_LEN = 32
D_CMP_STRIDE = 16
D_SLC_BLOCK = 64
D_SLC_TOPN = 16
D_SLC_LOCAL = 2
D_WINDOW = 512
D_SCALE = D_DIM ** -0.5

EVEN_SPLITS = (A_HEADS * A_DIM, A_DIM, A_DIM, B_QRANK, B_KVRANK, B_ROPE)
ODD_SPLITS = (C_HEADS * C_DIM, C_DIM, C_DIM, C_IDX_HEADS * C_IDX_DIM, C_IDX_DIM, C_IDX_HEADS,
              D_HEADS * D_DIM, 6 * D_DIM, 3 * D_HEADS)

LANE = 128
VMEM_LIMIT_BYTES = 56 * 1024 * 1024


def _cparams(*sem):
    return pltpu.CompilerParams(dimension_semantics=sem, vmem_limit_bytes=VMEM_LIMIT_BYTES)


def _pick_tile(n, cap, mult):
    best = None
    for t in range(mult, min(n, cap) + 1, mult):
        if n % t == 0:
            best = t
    assert best is not None, (n, cap, mult)
    return best


def _mm_body(x_ref, w_ref, o_ref, acc_ref, *, nk, act):
    k = pl.program_id(2)

    @pl.when(k == 0)
    def _():
        acc_ref[...] = jnp.zeros_like(acc_ref)

    acc_ref[...] += jnp.dot(x_ref[...].astype(jnp.bfloat16), w_ref[...],
                            preferred_element_type=jnp.float32)

    @pl.when(k == nk - 1)
    def _():
        r = acc_ref[...]
        if act == "relu2":
            r = jnp.maximum(r, 0.0)
            r = r * r
        o_ref[...] = r.astype(o_ref.dtype)


def matmul(x, w, *, act=None, out_dtype=jnp.float32):
    m, kdim = x.shape
    n = w.shape[1]
    tm = _pick_tile(m, 1024, 8)
    tn = _pick_tile(n, 2048, LANE)
    tk = _pick_tile(kdim, 512, LANE)
    nk = kdim // tk
    return pl.pallas_call(
        functools.partial(_mm_body, nk=nk, act=act),
        grid=(m // tm, n // tn, nk),
        in_specs=[pl.BlockSpec((tm, tk), lambda i, j, k: (i, k)),
                  pl.BlockSpec((tk, tn), lambda i, j, k: (k, j))],
        out_specs=pl.BlockSpec((tm, tn), lambda i, j, k: (i, j)),
        out_shape=jax.ShapeDtypeStruct((m, n), out_dtype),
        scratch_shapes=[pltpu.VMEM((tm, tn), jnp.float32)],
        compiler_params=_cparams("parallel", "parallel", "arbitrary"),
        name="matmul",
    )(x, w)


def _ln_body(x_ref, o_ref, g_ref, b_ref, y_ref, yb_ref):
    y = DN_ALPHA * x_ref[...] + o_ref[...]
    mu = jnp.mean(y, axis=-1, keepdims=True)
    yc = y - mu
    var = jnp.mean(yc * yc, axis=-1, keepdims=True)
    out = yc * lax.rsqrt(var + LN_EPS) * g_ref[...] + b_ref[...]
    y_ref[...] = out
    yb_ref[...] = out.astype(yb_ref.dtype)


def resid_layer_norm(x, o, g, b):
    m, d = x.shape
    tm = _pick_tile(m, 256, 8)
    row = pl.BlockSpec((tm, d), lambda i: (i, 0))
    vec = pl.BlockSpec((1, d), lambda i: (0, 0))
    return pl.pallas_call(
        _ln_body, grid=(m // tm,), in_specs=[row, row, vec, vec], out_specs=[row, row],
        out_shape=[jax.ShapeDtypeStruct((m, d), jnp.float32), jax.ShapeDtypeStruct((m, d), jnp.bfloat16)],
        compiler_params=_cparams("parallel"), name="resid_layer_norm",
    )(x, o, g.reshape(1, d), b.reshape(1, d))


NEG = -1e30
M_FLOOR = -1e29
BF = jnp.bfloat16
F32 = jnp.float32


def _dot(a, b):
    return jnp.dot(a, b, preferred_element_type=F32)


def _dot_nt(a, b):
    return lax.dot_general(a, b, (((1,), (1,)), ((), ())), preferred_element_type=F32)


def _rope(x, c, sm, sp, half):
    return x * c + pltpu.roll(x, LANE - half, 1) * sm + pltpu.roll(x, half, 1) * sp


def _add_head_mask(s, maskf, nh):
    tq, tk = maskf.shape
    return (s.reshape(nh, tq, tk) + maskf[None]).reshape(nh * tq, tk)


def _softmax_init(m_ref, l_ref, acc_ref):
    m_ref[...] = jnp.full(m_ref.shape, M_FLOOR, F32)
    l_ref[...] = jnp.zeros(l_ref.shape, F32)
    acc_ref[...] = jnp.zeros(acc_ref.shape, F32)


def _lane_partial_sum(p):
    tk = p.shape[1]
    if tk % LANE == 0:
        part = p[:, 0:LANE]
        for c in range(1, tk // LANE):
            part = part + p[:, c * LANE:(c + 1) * LANE]
        return part
    lane = lax.broadcasted_iota(jnp.int32, (p.shape[0], LANE), 1)
    return jnp.where(lane == 0, jnp.sum(p, axis=-1, keepdims=True), 0.0)


def _softmax_step(s, v, m_ref, l_ref, acc_ref, v_is_transposed=False):
    m_prev = m_ref[...]
    m_new = jnp.maximum(m_prev, jnp.max(s, axis=-1, keepdims=True))
    alpha = jnp.exp(m_prev - m_new)
    p = jnp.exp(s - m_new)
    l_ref[...] = alpha * l_ref[...] + _lane_partial_sum(p)
    pv = _dot_nt(p.astype(BF), v) if v_is_transposed else _dot(p.astype(BF), v)
    acc_ref[...] = alpha * acc_ref[...] + pv
    m_ref[...] = m_new


def _attend_tile(q_refs, score_fn, v, m_ref, l_ref, acc_ref, *, tq, heads_per_chunk, maskf=None, rowmask=None):
    ch = heads_per_chunk * tq
    n_chunks = q_refs[0].shape[0] // ch

    for i in range(n_chunks):
        rows = slice(i * ch, (i + 1) * ch)
        s = score_fn(*[r[rows, :] for r in q_refs])
        if maskf is not None:
            s = _add_head_mask(s, maskf, heads_per_chunk)
        if rowmask is not None:
            s = s + rowmask[rows]
        m_prev = m_ref[rows, :]
        m_new = jnp.maximum(m_prev, jnp.max(s, axis=-1, keepdims=True))
        alpha = jnp.exp(m_prev - m_new)
        p = jnp.concatenate([jnp.exp(s[:, c * LANE:(c + 1) * LANE] - m_new) for c in range(s.shape[1] // LANE)], axis=1)
        l_ref[rows, :] = alpha * l_ref[rows, :] + _lane_partial_sum(p)
        dv = acc_ref.shape[1]
        alpha_v = alpha if dv == LANE else jnp.concatenate([alpha] * (dv // LANE), axis=1)
        acc_ref[rows, :] = alpha_v * acc_ref[rows, :] + _dot(p.astype(BF), v)
        m_ref[rows, :] = m_new


def _softmax_finish(l_ref, acc_ref):
    l = jnp.sum(l_ref[...], axis=-1, keepdims=True)
    return acc_ref[...] / jnp.where(l > 0.0, l, 1.0)


def _top_lanes(g, avail, k):
    lane = lax.broadcasted_iota(jnp.int32, g.shape, 1)
    sel = jnp.zeros(g.shape, jnp.bool_)
    for _ in range(k):
        gm = jnp.where(avail, g, -jnp.inf)
        mx = jnp.max(gm, axis=-1, keepdims=True)
        idx = jnp.min(jnp.where(avail & (gm == mx), lane, jnp.int32(2 ** 30)), axis=-1, keepdims=True)
        pick = lane == idx
        sel = sel | pick
        avail = avail & jnp.logical_not(pick)
    return sel


def _rope_tables(pos, rot_dim, theta, width):
    half = rot_dim // 2
    inv = theta ** (-jnp.arange(0, rot_dim, 2, dtype=jnp.float32) / rot_dim)
    ang = pos.astype(jnp.float32)[:, None] * inv[None, :]
    c, s = jnp.cos(ang), jnp.sin(ang)
    t = pos.shape[0]
    one = jnp.ones((t, width - rot_dim), F32)
    z_half = jnp.zeros((t, half), F32)
    z_rest = jnp.zeros((t, width - rot_dim), F32)
    rep = LANE // width
    cc = jnp.tile(jnp.concatenate([c, c, one], axis=1), (1, rep))
    sm = jnp.tile(jnp.concatenate([-s, z_half, z_rest], axis=1), (1, rep))
    sp = jnp.tile(jnp.concatenate([z_half, s, z_rest], axis=1), (1, rep))
    return cc, sm, sp


def _tok_spec(tm, w):
    return pl.BlockSpec((tm, w), lambda i: (i, 0))


def _full_spec(shape):
    nd = len(shape)
    return pl.BlockSpec(shape, lambda *_: (0,) * nd)


_E_AK, _E_AV, _E_CQ, _E_CKV, _E_KR = 2048, 2176, 2304, 3072, 3328


def _prep_even_body(h_ref, ca, sma, spa, cb, smb, spb, gq_ref, gkv_ref,
                    aq_ref, arows_ref, cqn_ref, brows_ref):
    c, sm, sp = ca[...], sma[...], spa[...]
    ha = A_DIM // 8
    for h in range(A_HEADS):
        aq_ref[:, h * LANE:(h + 1) * LANE] = _rope(h_ref[:, h * LANE:(h + 1) * LANE], c, sm, sp, ha)
    arows_ref[:, 0:A_DIM] = _rope(h_ref[:, _E_AK:_E_AK + A_DIM], c, sm, sp, ha)
    arows_ref[:, A_DIM:2 * A_DIM] = h_ref[:, _E_AV:_E_AV + A_DIM]
    cq = h_ref[:, _E_CQ:_E_CQ + B_QRANK]
    cqn = cq * lax.rsqrt(jnp.mean(cq * cq, axis=-1, keepdims=True) + RMS_EPS) * gq_ref[...]
    cqn_ref[...] = cqn.astype(cqn_ref.dtype)
    ckv = h_ref[:, _E_CKV:_E_CKV + B_KVRANK]
    brows_ref[:, 0:B_KVRANK] = ckv * lax.rsqrt(jnp.mean(ckv * ckv, axis=-1, keepdims=True) + RMS_EPS) * gkv_ref[...]
    kr = _rope(h_ref[:, _E_KR:_E_KR + LANE], cb[...], smb[...], spb[...], B_ROPE // 2)
    brows_ref[:, B_KVRANK:B_KVRANK + B_ROPE] = kr[:, 0:B_ROPE]


def prep_even(h, tab_a, tab_b, g_cq, g_ckv):
    t, w = h.shape
    tm = _pick_tile(t, 256, 8)
    tab = _tok_spec(tm, LANE)
    return pl.pallas_call(
        _prep_even_body, grid=(t // tm,),
        in_specs=[_tok_spec(tm, w), tab, tab, tab, tab, tab, tab,
                  _full_spec((1, B_QRANK)), _full_spec((1, B_KVRANK))],
        out_specs=[_tok_spec(tm, A_HEADS * A_DIM), _tok_spec(tm, A_ROW),
                   _tok_spec(tm, B_QRANK), _tok_spec(tm, B_KVRANK + B_ROPE)],
        out_shape=[jax.ShapeDtypeStruct((t, A_HEADS * A_DIM), F32),
                   jax.ShapeDtypeStruct((t, A_ROW), F32),
                   jax.ShapeDtypeStruct((t, B_QRANK), BF),
                   jax.ShapeDtypeStruct((t, B_KVRANK + B_ROPE), F32)],
        compiler_params=_cparams("parallel"), name="prep_even",
    )(h, *tab_a, *tab_b, g_cq.reshape(1, -1), g_ckv.reshape(1, -1))


def _mla_qprep_body(q_ref, wuk_ref, cb, smb, spb, ql_ref, qr_ref):
    for h in range(B_HEADS):
        qn = q_ref[:, h * B_NOPE:(h + 1) * B_NOPE].astype(BF)
        ql_ref[:, h * B_KVRANK:(h + 1) * B_KVRANK] = _dot(qn, wuk_ref[h])
    c, sm, sp = cb[...], smb[...], spb[...]
    lane = lax.broadcasted_iota(jnp.int32, c.shape, 1)
    base = B_HEADS * B_NOPE
    for j in range(B_HEADS // 2):
        r = _rope(q_ref[:, base + j * LANE:base + (j + 1) * LANE], c, sm, sp, B_ROPE // 2)
        qr_ref[:, (2 * j) * LANE:(2 * j + 1) * LANE] = jnp.where(lane < B_ROPE, r, 0.0)
        qr_ref[:, (2 * j + 1) * LANE:(2 * j + 2) * LANE] = jnp.where(lane < B_ROPE, pltpu.roll(r, B_ROPE, 1), 0.0)


def mla_qprep(q, wuk_t, tab_b):
    t, w = q.shape
    tm = _pick_tile(t, 256, 8)
    tab = _tok_spec(tm, LANE)
    return pl.pallas_call(
        _mla_qprep_body, grid=(t // tm,),
        in_specs=[_tok_spec(tm, w), _full_spec(wuk_t.shape), tab, tab, tab],
        out_specs=[_tok_spec(tm, B_HEADS * B_KVRANK), _tok_spec(tm, B_HEADS * LANE)],
        out_shape=[jax.ShapeDtypeStruct((t, B_HEADS * B_KVRANK), F32),
                   jax.ShapeDtypeStruct((t, B_HEADS * LANE), F32)],
        compiler_params=_cparams("parallel"), name="mla_qprep",
    )(q, wuk_t, *tab_b)


def _mla_oproj_body(ol_ref, wuv_ref, o_ref):
    for h in range(B_HEADS):
        ol = ol_ref[:, h * B_KVRANK:(h + 1) * B_KVRANK].astype(BF)
        o_ref[:, h * B_VDIM:(h + 1) * B_VDIM] = _dot(ol, wuv_ref[h]).astype(o_ref.dtype)


def mla_oproj(o_lat, wuv):
    t, w = o_lat.shape
    tm = _pick_tile(t, 256, 8)
    return pl.pallas_call(
        _mla_oproj_body, grid=(t // tm,),
        in_specs=[_tok_spec(tm, w), _full_spec(wuv.shape)],
        out_specs=_tok_spec(tm, B_HEADS * B_VDIM),
        out_shape=jax.ShapeDtypeStruct((t, B_HEADS * B_VDIM), BF),
        compiler_params=_cparams("parallel"), name="mla_oproj",
    )(o_lat, wuv)


def _pages_per_step(n_pages, cap=16):
    return _pick_tile(n_pages, cap, 1)


def _page_specs(g_pages, width, col_block):
    def spec(g):
        return pl.BlockSpec((None, PAGE_SIZE, width),
                            lambda b, s, pt: (pt[b, s * g_pages + g], 0, col_block))
    return [spec(g) for g in range(g_pages)]


def _cat_pages(pages, lo, hi):
    return jnp.concatenate([p[:, lo:hi] for p in pages], axis=0)


def _page_specs_t(g_pages, feats, feat_block):
    def spec(g):
        return pl.BlockSpec((None, feats, PAGE_SIZE),
                            lambda b, s, pt: (pt[b, s * g_pages + g], feat_block, 0))
    return [spec(g) for g in range(g_pages)]


def _cat_pages_t(pages, lo, hi):
    return jnp.concatenate([p[lo:hi, :] for p in pages], axis=1)


_MOBA_HEADS_PER_CHUNK = 4


def _moba_p_body(aq_ref, rows_ref, o_ref, qs, m_ref, l_ref, acc_ref, *, tq, sp):
    nh = A_HEADS
    qi = pl.program_id(1)
    q0 = qi * tq
    own = q0 // A_BLOCK
    nblk = sp // A_BLOCK
    for h in range(nh):
        qs[h * tq:(h + 1) * tq, :] = aq_ref[:, h * A_DIM:(h + 1) * A_DIM].astype(BF)
    q = qs[...]
    km = [jnp.mean(rows_ref[n * A_BLOCK:(n + 1) * A_BLOCK, 0:A_DIM], axis=0, keepdims=True) for n in range(nblk)]
    km = jnp.concatenate(km + [jnp.zeros((LANE - nblk, A_DIM), F32)], axis=0)
    gate = _dot_nt(q, km.astype(BF))
    lane = lax.broadcasted_iota(jnp.int32, gate.shape, 1)
    selm = jnp.where(_top_lanes(gate, lane < own, A_TOPK), 0.0, NEG)
    _softmax_init(m_ref, l_ref, acc_ref)
    for n in range(nblk):
        @pl.when(n <= own)
        def _():
            k = rows_ref[n * A_BLOCK:(n + 1) * A_BLOCK, 0:A_DIM].astype(BF)
            v = rows_ref[n * A_BLOCK:(n + 1) * A_BLOCK, A_DIM:2 * A_DIM].astype(BF)
            qpos = q0 + lax.broadcasted_iota(jnp.int32, (tq, A_BLOCK), 0)
            kpos = n * A_BLOCK + lax.broadcasted_iota(jnp.int32, (tq, A_BLOCK), 1)
            is_own = n == own
            _attend_tile([qs], lambda qc: _dot_nt(qc, k) * A_SCALE, v, m_ref, l_ref, acc_ref,
                         tq=tq, heads_per_chunk=_MOBA_HEADS_PER_CHUNK,
                         maskf=jnp.where(is_own & (kpos > qpos), NEG, 0.0),
                         rowmask=jnp.where(is_own, 0.0, selm[:, n:n + 1]))
    res = _softmax_finish(l_ref, acc_ref)
    for h in range(nh):
        o_ref[:, h * A_DIM:(h + 1) * A_DIM] = res[h * tq:(h + 1) * tq, :].astype(o_ref.dtype)


def moba_prompt(aq, a_rows, nb, sp):
    tq = _pick_tile(sp, 128, 8)
    nq = sp // tq
    m = A_HEADS * tq
    return pl.pallas_call(
        functools.partial(_moba_p_body, tq=tq, sp=sp), grid=(nb, nq),
        in_specs=[pl.BlockSpec((tq, A_HEADS * A_DIM), lambda b, i: (b * nq + i, 0)),
                  pl.BlockSpec((sp, A_ROW), lambda b, i: (b, 0))],
        out_specs=pl.BlockSpec((tq, A_HEADS * A_DIM), lambda b, i: (b * nq + i, 0)),
        out_shape=jax.ShapeDtypeStruct((nb * sp, A_HEADS * A_DIM), BF),
        scratch_shapes=[pltpu.VMEM((m, A_DIM), BF), pltpu.VMEM((m, LANE), F32), pltpu.VMEM((m, LANE), F32),
                        pltpu.VMEM((m, A_DIM), F32)],
        compiler_params=_cparams("parallel", "arbitrary"), name="moba_prompt",
    )(aq, a_rows)


def _moba_s_body(pt_ref, aq_ref, new_ref, *rest, g_pages, steps, ss):
    pages = rest[:g_pages]
    o_ref, qs, gs, ms, ls, accs = rest[g_pages:]
    nh = A_HEADS
    m_rows = nh * ss
    step = pl.program_id(1)
    bps = g_pages * PAGE_SIZE // A_BLOCK
    ppb = A_BLOCK // PAGE_SIZE

    @pl.when(step == 0)
    def _():
        for h in range(nh):
            qs[h * ss:(h + 1) * ss, :] = aq_ref[:, h * A_DIM:(h + 1) * A_DIM]
        gs[...] = jnp.full(gs.shape, NEG, F32)
        ms[...] = jnp.full(ms.shape, NEG, F32)
        ls[...] = jnp.zeros(ls.shape, F32)

    q = qs[...].astype(BF)
    lane = lax.broadcasted_iota(jnp.int32, (m_rows, LANE), 1)
    for i in range(bps):
        n = step * bps + i
        blk = pages[i * ppb:(i + 1) * ppb]
        k = _cat_pages(blk, 0, A_DIM).astype(BF)
        v = _cat_pages(blk, A_DIM, 2 * A_DIM).astype(BF)
        s = _dot_nt(q, k) * A_SCALE
        m_n = jnp.max(s, axis=-1, keepdims=True)
        p = jnp.exp(s - m_n)
        here = lane == n
        gs[...] = jnp.where(here, jnp.mean(s, axis=-1, keepdims=True), gs[...])
        ms[...] = jnp.where(here, m_n, ms[...])
        ls[...] = jnp.where(here, jnp.sum(p, axis=-1, keepdims=True), ls[...])
        accs[n] = _dot(p.astype(BF), v)

    @pl.when(step == steps - 1)
    def _():
        nblk = steps * bps
        sel = _top_lanes(gs[...], lane < nblk, A_TOPK)
        kn = new_ref[:, 0:A_DIM].astype(BF)
        vn = new_ref[:, A_DIM:2 * A_DIM].astype(BF)
        qrow = lax.broadcasted_iota(jnp.int32, (ss, ss), 0)
        kcol = lax.broadcasted_iota(jnp.int32, (ss, ss), 1)
        s_own = _add_head_mask(_dot_nt(q, kn) * A_SCALE, jnp.where(kcol <= qrow, 0.0, NEG), nh)
        m_own = jnp.max(s_own, axis=-1, keepdims=True)
        p_own = jnp.where(s_own > 0.5 * NEG, jnp.exp(s_own - m_own), 0.0)
        m_fin = jnp.maximum(jnp.max(jnp.where(sel, ms[...], NEG), axis=-1, keepdims=True), m_own)
        w = jnp.where(sel, jnp.exp(ms[...] - m_fin), 0.0)
        w_own = jnp.exp(m_own - m_fin)
        l_fin = jnp.sum(w * ls[...], axis=-1, keepdims=True) + w_own * jnp.sum(p_own, axis=-1, keepdims=True)
        acc = w_own * _dot(p_own.astype(BF), vn)
        for n in range(nblk):
            acc = acc + w[:, n:n + 1] * accs[n]
        res = acc / l_fin
        for h in range(nh):
            o_ref[:, h * A_DIM:(h + 1) * A_DIM] = res[h * ss:(h + 1) * ss, :].astype(o_ref.dtype)


def moba_sample(aq, a_rows, pool, page_table, tp, ns, ss):
    n_pages = page_table.shape[1]
    g_pages = _pages_per_step(n_pages)
    steps = n_pages // g_pages
    assert g_pages % (A_BLOCK // PAGE_SIZE) == 0 and n_pages * PAGE_SIZE // A_BLOCK <= LANE
    m = A_HEADS * ss
    tok = lambda w: pl.BlockSpec((ss, w), lambda b, s, pt: (tp // ss + b, 0))
    grid_spec = pltpu.PrefetchScalarGridSpec(
        num_scalar_prefetch=1, grid=(ns, steps),
        in_specs=[tok(A_HEADS * A_DIM), tok(A_ROW)] + _page_specs(g_pages, A_ROW, 0),
        out_specs=pl.BlockSpec((ss, A_HEADS * A_DIM), lambda b, s, pt: (b, 0)),
        scratch_shapes=[pltpu.VMEM((m, A_DIM), F32), pltpu.VMEM((m, LANE), F32), pltpu.VMEM((m, LANE), F32),
                        pltpu.VMEM((m, LANE), F32),
                        pltpu.VMEM((n_pages * PAGE_SIZE // A_BLOCK, m, A_DIM), F32)])
    return pl.pallas_call(
        functools.partial(_moba_s_body, g_pages=g_pages, steps=steps, ss=ss), grid_spec=grid_spec,
        out_shape=jax.ShapeDtypeStruct((ns * ss, A_HEADS * A_DIM), BF),
        compiler_params=_cparams("parallel", "arbitrary"), name="moba_sample",
    )(page_table, aq, a_rows, *([pool] * g_pages))


_MLA_HEADS_PER_CHUNK = 4
_DSA_HEADS_PER_CHUNK = 4


def _mla_scores(qls, qrs, blk):
    ckv = blk[:, 0:B_KVRANK].astype(BF)
    kr = blk[:, B_KVRANK:B_KVRANK + B_ROPE].astype(BF)
    return (_dot_nt(qls, ckv) + _dot_nt(qrs, kr)) * B_SCALE, ckv


def _mla_p_body(ql_ref, qr_ref, rows_ref, o_ref, qls, qrs, m_ref, l_ref, acc_ref, *, tq, tk):
    nh = B_HEADS
    qi = pl.program_id(1)
    q0 = qi * tq
    for h in range(nh):
        qls[h * tq:(h + 1) * tq, :] = ql_ref[:, h * B_KVRANK:(h + 1) * B_KVRANK].astype(BF)
        qrs[h * tq:(h + 1) * tq, :] = qr_ref[:, h * LANE:h * LANE + B_ROPE].astype(BF)
    _softmax_init(m_ref, l_ref, acc_ref)

    def body(j, carry):
        k0 = pl.multiple_of(j * tk, tk)
        blk = rows_ref[pl.ds(k0, tk), :]
        ckv = blk[:, 0:B_KVRANK].astype(BF)
        kr = blk[:, B_KVRANK:B_KVRANK + B_ROPE].astype(BF)
        qpos = q0 + lax.broadcasted_iota(jnp.int32, (tq, tk), 0)
        kpos = k0 + lax.broadcasted_iota(jnp.int32, (tq, tk), 1)
        _attend_tile([qls, qrs], lambda ql, qr: (_dot_nt(ql, ckv) + _dot_nt(qr, kr)) * B_SCALE, ckv,
                     m_ref, l_ref, acc_ref, tq=tq, heads_per_chunk=_MLA_HEADS_PER_CHUNK,
                     maskf=jnp.where(kpos <= qpos, 0.0, NEG))
        return carry

    lax.fori_loop(0, (q0 + tq + tk - 1) // tk, body, 0)
    res = _softmax_finish(l_ref, acc_ref)
    for h in range(nh):
        o_ref[:, h * B_KVRANK:(h + 1) * B_KVRANK] = res[h * tq:(h + 1) * tq, :]


def mla_prompt(q_lat, q_rope, b_rows, nb, sp):
    tq = _pick_tile(sp, 128, 8)
    tk = _pick_tile(sp, 256, 8)
    nq = sp // tq
    m = B_HEADS * tq
    tok = lambda w: pl.BlockSpec((tq, w), lambda b, i: (b * nq + i, 0))
    return pl.pallas_call(
        functools.partial(_mla_p_body, tq=tq, tk=tk), grid=(nb, nq),
        in_specs=[tok(B_HEADS * B_KVRANK), tok(B_HEADS * LANE),
                  pl.BlockSpec((sp, B_KVRANK + B_ROPE), lambda b, i: (b, 0))],
        out_specs=tok(B_HEADS * B_KVRANK),
        out_shape=jax.ShapeDtypeStruct((nb * sp, B_HEADS * B_KVRANK), F32),
        scratch_shapes=[pltpu.VMEM((m, B_KVRANK), BF), pltpu.VMEM((m, B_ROPE), BF),
                        pltpu.VMEM((m, LANE), F32), pltpu.VMEM((m, LANE), F32), pltpu.VMEM((m, B_KVRANK), F32)],
        compiler_params=_cparams("parallel", "arbitrary"), name="mla_prompt",
    )(q_lat, q_rope, b_rows)


def _mla_s_body(pt_ref, ql_ref, qr_ref, new_ref, *rest, g_pages, steps, ss):
    pages = rest[:g_pages]
    o_ref, qls, qrs, m_ref, l_ref, acc_ref = rest[g_pages:]
    nh = B_HEADS
    step = pl.program_id(1)

    @pl.when(step == 0)
    def _():
        for h in range(nh):
            qls[h * ss:(h + 1) * ss, :] = ql_ref[:, h * B_KVRANK:(h + 1) * B_KVRANK]
            qrs[h * ss:(h + 1) * ss, :] = qr_ref[:, h * LANE:(h + 1) * LANE]
        _softmax_init(m_ref, l_ref, acc_ref)

    ql = qls[...].astype(BF)
    qr = qrs[:, 0:B_ROPE].astype(BF)
    ckv_t = _cat_pages_t(pages, 0, B_KVRANK).astype(BF)
    kr_t = _cat_pages_t(pages, B_KVRANK, B_KVRANK + B_ROPE).astype(BF)
    s = (_dot(ql, ckv_t) + _dot(qr, kr_t)) * B_SCALE
    _softmax_step(s, ckv_t, m_ref, l_ref, acc_ref, v_is_transposed=True)

    @pl.when(step == steps - 1)
    def _():
        s_new, ckv_new = _mla_scores(ql, qr, new_ref[...])
        qrow = lax.broadcasted_iota(jnp.int32, (ss, ss), 0)
        kcol = lax.broadcasted_iota(jnp.int32, (ss, ss), 1)
        _softmax_step(_add_head_mask(s_new, jnp.where(kcol <= qrow, 0.0, NEG), nh), ckv_new, m_ref, l_ref, acc_ref)
        res = _softmax_finish(l_ref, acc_ref)
        for h in range(nh):
            o_ref[:, h * B_KVRANK:(h + 1) * B_KVRANK] = res[h * ss:(h + 1) * ss, :]


def mla_sample(q_lat, q_rope, b_rows, pool, page_table, tp, ns, ss):
    n_pages = page_table.shape[1]
    g_pages = _pages_per_step(n_pages)
    steps = n_pages // g_pages
    m = B_HEADS * ss
    row_w = B_KVRANK + B_ROPE
    tok = lambda w: pl.BlockSpec((ss, w), lambda b, s, pt: (tp // ss + b, 0))
    grid_spec = pltpu.PrefetchScalarGridSpec(
        num_scalar_prefetch=1, grid=(ns, steps),
        in_specs=[tok(B_HEADS * B_KVRANK), tok(B_HEADS * LANE), tok(row_w)] + _page_specs_t(g_pages, row_w, 0),
        out_specs=pl.BlockSpec((ss, B_HEADS * B_KVRANK), lambda b, s, pt: (b, 0)),
        scratch_shapes=[pltpu.VMEM((m, B_KVRANK), F32), pltpu.VMEM((m, LANE), F32),
                        pltpu.VMEM((m, 1), F32), pltpu.VMEM((m, LANE), F32), pltpu.VMEM((m, B_KVRANK), F32)])
    return pl.pallas_call(
        functools.partial(_mla_s_body, g_pages=g_pages, steps=steps, ss=ss), grid_spec=grid_spec,
        out_shape=jax.ShapeDtypeStruct((ns * ss, B_HEADS * B_KVRANK), F32),
        compiler_params=_cparams("parallel", "arbitrary"), name="mla_sample",
    )(page_table, q_lat, q_rope, b_rows, *([jnp.swapaxes(pool, 1, 2)] * g_pages))


def even_mixers(h, pos, tp, nb, sp, ns, ss, pool_a, pool_b, page_table, g_cq, g_ckv, w_uq, w_ukv):
    tab_a = _rope_tables(pos, A_DIM // 4, ROPE_THETA, A_DIM)
    tab_b = _rope_tables(pos, B_ROPE, B_ROPE_THETA, B_ROPE)
    aq, a_rows, cqn, b_rows = prep_even(h, tab_a, tab_b, g_cq, g_ckv)
    w_uq_cols = jnp.concatenate([w_uq[:, :, :B_NOPE].reshape(B_QRANK, -1),
                                 w_uq[:, :, B_NOPE:].reshape(B_QRANK, -1)], axis=1).astype(BF)
    q = matmul(cqn, w_uq_cols)
    wuk_t = jnp.transpose(w_ukv[:, :, :B_NOPE], (1, 2, 0)).astype(BF)
    wuv = jnp.transpose(w_ukv[:, :, B_NOPE:], (1, 0, 2)).astype(BF)
    q_lat, q_rope = mla_qprep(q, wuk_t, tab_b)
    o_a = jnp.concatenate([moba_prompt(aq, a_rows, nb, sp),
                           moba_sample(aq, a_rows, pool_a, page_table, tp, ns, ss)], axis=0)
    o_lat = jnp.concatenate([mla_prompt(q_lat, q_rope, b_rows, nb, sp),
                             mla_sample(q_lat, q_rope, b_rows, pool_b, page_table, tp, ns, ss)], axis=0)
    o_b = mla_oproj(o_lat, wuv)
    return jnp.concatenate([o_a, o_b], axis=1), a_rows, b_rows


_O_CK, _O_CV, _O_IQ, _O_IK, _O_IW, _O_DQ, _O_DKV, _O_DG, _O_END = 2048, 2176, 2304, 4352, 4480, 4608, 6656, 7040, 7168
INT_MIN = -2 ** 31


def _odd_weight_cols(w):
    cuts = [int(c) for c in np.cumsum(ODD_SPLITS)[:-1]]
    cq, ck, cv, iq, ik, iw, dq, dkv, dg = jnp.split(w, cuts, axis=1)
    dg = dg.reshape(-1, D_HEADS, 3).transpose(0, 2, 1).reshape(-1, 3 * D_HEADS)
    pad = lambda a, n: jnp.pad(a, ((0, 0), (0, n - a.shape[1])))
    return jnp.concatenate([cq, ck, cv, iq, pad(ik, LANE), pad(iw, LANE), dq, dkv, pad(dg, LANE)], axis=1)


def _split_heads_64(r, lane):
    return jnp.where(lane < 64, r, 0.0), jnp.where(lane < 64, pltpu.roll(r, 64, 1), 0.0)


def _prep_odd_body(h_ref, ca, sma, spa, ci, smi, spi, cq_ref, crows_ref, iq_ref, iw_ref, dq_ref, drows_ref,
                   win_ref, gate_ref):
    c, sm, sp = ca[...], sma[...], spa[...]
    ha = C_DIM // 8
    for h in range(C_HEADS):
        cq_ref[:, h * LANE:(h + 1) * LANE] = _rope(h_ref[:, h * LANE:(h + 1) * LANE], c, sm, sp, ha)
    crows_ref[:, 0:C_DIM] = _rope(h_ref[:, _O_CK:_O_CK + C_DIM], c, sm, sp, ha)
    crows_ref[:, C_DIM:2 * C_DIM] = h_ref[:, _O_CV:_O_CV + C_DIM]
    c, sm, sp = ci[...], smi[...], spi[...]
    hi = C_IDX_DIM // 8
    lane = lax.broadcasted_iota(jnp.int32, c.shape, 1)
    ik = _rope(h_ref[:, _O_IK:_O_IK + LANE], c, sm, sp, hi)
    crows_ref[:, 2 * C_DIM:2 * C_DIM + C_IDX_DIM] = ik[:, 0:C_IDX_DIM]
    for j in range(C_IDX_HEADS // 2):
        a, b = _split_heads_64(_rope(h_ref[:, _O_IQ + j * LANE:_O_IQ + (j + 1) * LANE], c, sm, sp, hi), lane)
        iq_ref[:, (2 * j) * LANE:(2 * j + 1) * LANE] = a
        iq_ref[:, (2 * j + 1) * LANE:(2 * j + 2) * LANE] = b
    iw_ref[...] = h_ref[:, _O_IW:_O_IW + LANE] * C_IDX_W_SCALE
    for j in range(D_HEADS // 2):
        a, b = _split_heads_64(_rope(h_ref[:, _O_DQ + j * LANE:_O_DQ + (j + 1) * LANE], c, sm, sp, hi) * D_SCALE, lane)
        dq_ref[:, (2 * j) * LANE:(2 * j + 1) * LANE] = a
        dq_ref[:, (2 * j + 1) * LANE:(2 * j + 2) * LANE] = b
    left = lane < D_DIM
    c1, sm1, sp1 = jnp.where(left, c, 1.0), jnp.where(left, sm, 0.0), jnp.where(left, sp, 0.0)
    drows_ref[:, 0:LANE] = h_ref[:, _O_DKV:_O_DKV + LANE]
    drows_ref[:, LANE:2 * LANE] = _rope(h_ref[:, _O_DKV + LANE:_O_DKV + 2 * LANE], c1, sm1, sp1, hi)
    win_ref[...] = _rope(h_ref[:, _O_DKV + 2 * LANE:_O_DKV + 3 * LANE], c1, sm1, sp1, hi)
    gate_ref[...] = jax.nn.sigmoid(h_ref[:, _O_DG:_O_DG + LANE])


def prep_odd(h, tab_a, tab_i):
    t, w = h.shape
    tm = _pick_tile(t, 256, 8)
    tab = _tok_spec(tm, LANE)
    widths = [C_HEADS * C_DIM, 2 * C_DIM + C_IDX_DIM, C_IDX_HEADS * LANE, LANE, D_HEADS * LANE, 4 * D_DIM, LANE, LANE]
    return pl.pallas_call(
        _prep_odd_body, grid=(t // tm,),
        in_specs=[_tok_spec(tm, w), tab, tab, tab, tab, tab, tab],
        out_specs=[_tok_spec(tm, wd) for wd in widths],
        out_shape=[jax.ShapeDtypeStruct((t, wd), F32) for wd in widths],
        compiler_params=_cparams("parallel"), name="prep_odd",
    )(h, *tab_a, *tab_i)


def _sort_key(score, adm):
    b = lax.bitcast_convert_type(score + 0.0, jnp.int32)
    key = b ^ ((b >> 31) & jnp.int32(0x7FFFFFFF))
    return jnp.where(adm, key, jnp.int32(INT_MIN))


def _kth_largest(count_ge, shape, k):
    def body(it, ans):
        cand = ans + jnp.left_shift(jnp.int32(1), 31 - it)
        return jnp.where(count_ge(cand) >= k, cand, ans)
    return lax.fori_loop(0, 32, body, jnp.full(shape, INT_MIN, jnp.int32))


def _index_scores(iq_heads, iw, ikt):
    score = None
    for h in range(C_IDX_HEADS):
        rel = jnp.maximum(_dot_nt(iq_heads(h), ikt), 0.0)
        term = iw[:, h:h + 1] * rel
        score = term if score is None else score + term
    return score


def _tie_prefix(tie):
    r = lax.broadcasted_iota(jnp.int32, (LANE, LANE), 0)
    c = lax.broadcasted_iota(jnp.int32, (LANE, LANE), 1)
    upper = jnp.where(r < c, 1.0, 0.0).astype(BF)
    return _dot(jnp.where(tie, 1.0, 0.0).astype(BF), upper)


def _dsa_p_body(cq_ref, iq_ref, iw_ref, rows_ref, o_ref, qs, sck, m_ref, l_ref, acc_ref, *, tq, tk, sp, kk):
    nh = C_HEADS
    qi = pl.program_id(1)
    q0 = qi * tq
    ntile = sp // tk
    for h in range(nh):
        qs[h * tq:(h + 1) * tq, :] = cq_ref[:, h * C_DIM:(h + 1) * C_DIM].astype(BF)
    iw = iw_ref[...]
    qpos = q0 + lax.broadcasted_iota(jnp.int32, (tq, tk), 0)
    lane_k = lax.broadcasted_iota(jnp.int32, (tq, tk), 1)
    for j in range(ntile):
        @pl.when(j * tk < q0 + tq)
        def _():
            ikt = rows_ref[j * tk:(j + 1) * tk, 2 * C_DIM:2 * C_DIM + C_IDX_DIM].astype(BF)
            score = _index_scores(lambda h: iq_ref[:, h * LANE:h * LANE + C_IDX_DIM].astype(BF), iw, ikt)
            sck[:, j * tk:(j + 1) * tk] = _sort_key(score, j * tk + lane_k <= qpos)

        @pl.when(j * tk >= q0 + tq)
        def _():
            sck[:, j * tk:(j + 1) * tk] = jnp.full((tq, tk), INT_MIN, jnp.int32)

    def count_ge(t):
        return jnp.sum(jnp.where(sck[...] >= t, 1.0, 0.0), axis=-1, keepdims=True)

    thr = _kth_largest(count_ge, (tq, 1), float(kk))
    n_gt = jnp.sum(jnp.where(sck[...] > thr, 1.0, 0.0), axis=-1, keepdims=True)
    need = float(kk) - n_gt
    excess = (count_ge(thr) - n_gt > need) & (thr > INT_MIN)

    @pl.when(jnp.max(jnp.where(excess, 1.0, 0.0)) > 0.0)
    def _():
        carry = jnp.zeros((tq, 1), F32)
        for c in range(sp // LANE):
            keys = sck[:, c * LANE:(c + 1) * LANE]
            tie = keys == thr
            drop = excess & tie & (carry + _tie_prefix(tie) >= need)
            sck[:, c * LANE:(c + 1) * LANE] = jnp.where(drop, jnp.int32(INT_MIN), keys)
            carry = carry + jnp.sum(jnp.where(tie, 1.0, 0.0), axis=-1, keepdims=True)

    thr = jnp.maximum(thr, INT_MIN + 1)
    _softmax_init(m_ref, l_ref, acc_ref)
    for j in range(ntile):
        @pl.when(j * tk < q0 + tq)
        def _():
            k = rows_ref[j * tk:(j + 1) * tk, 0:C_DIM].astype(BF)
            v = rows_ref[j * tk:(j + 1) * tk, C_DIM:2 * C_DIM].astype(BF)
            maskf = jnp.where(sck[:, j * tk:(j + 1) * tk] >= thr, 0.0, NEG)
            _attend_tile([qs], lambda qc: _dot_nt(qc, k) * C_SCALE, v, m_ref, l_ref, acc_ref,
                         tq=tq, heads_per_chunk=_DSA_HEADS_PER_CHUNK, maskf=maskf)
    res = _softmax_finish(l_ref, acc_ref)
    for h in range(nh):
        o_ref[:, h * C_DIM:(h + 1) * C_DIM] = res[h * tq:(h + 1) * tq, :].astype(o_ref.dtype)


def dsa_prompt(cq, iq, iw, c_rows, nb, sp):
    tq = _pick_tile(sp, 128, 8)
    tk = _pick_tile(sp, 512, LANE)
    nq = sp // tq
    m = C_HEADS * tq
    tok = lambda w: pl.BlockSpec((tq, w), lambda b, i: (b * nq + i, 0))
    return pl.pallas_call(
        functools.partial(_dsa_p_body, tq=tq, tk=tk, sp=sp, kk=min(C_TOPK, sp // 4)), grid=(nb, nq),
        in_specs=[tok(C_HEADS * C_DIM), tok(C_IDX_HEADS * LANE), tok(LANE),
                  pl.BlockSpec((sp, 2 * C_DIM + C_IDX_DIM), lambda b, i: (b, 0))],
        out_specs=tok(C_HEADS * C_DIM),
        out_shape=jax.ShapeDtypeStruct((nb * sp, C_HEADS * C_DIM), BF),
        scratch_shapes=[pltpu.VMEM((m, C_DIM), BF), pltpu.VMEM((tq, sp), jnp.int32),
                        pltpu.VMEM((m, LANE), F32), pltpu.VMEM((m, LANE), F32), pltpu.VMEM((m, C_DIM), F32)],
        compiler_params=_cparams("parallel", "arbitrary"), name="dsa_prompt",
    )(cq, iq, iw, c_rows)


def _dsa_si_body(pt_ref, iq_ref, iw_ref, new_ref, *rest, g_pages, steps, ss, kk):
    pages = rest[:g_pages]
    mask_ref, iqs, sc = rest[g_pages:]
    step = pl.program_id(1)
    w = g_pages * PAGE_SIZE
    nhi = C_IDX_HEADS

    @pl.when(step == 0)
    def _():
        for h in range(nhi):
            iqs[h * ss:(h + 1) * ss, :] = iq_ref[:, h * LANE:(h + 1) * LANE]

    iw = iw_ref[...]
    iq_all = iqs[:, 0:C_IDX_DIM].astype(BF)

    def scores(rel):
        acc = iw[:, 0:1] * rel[0:ss]
        for h in range(1, nhi):
            acc = acc + iw[:, h:h + 1] * rel[h * ss:(h + 1) * ss]
        return acc

    ik_t = _cat_pages_t(pages, 0, C_IDX_DIM).astype(BF)
    sc[step] = _sort_key(scores(jnp.maximum(_dot(iq_all, ik_t), 0.0)), jnp.full((ss, w), True))

    @pl.when(step == steps - 1)
    def _():
        ikn = new_ref[:, 2 * C_DIM:2 * C_DIM + C_IDX_DIM].astype(BF)
        s_new = scores(jnp.maximum(_dot_nt(iq_all, ikn), 0.0))
        qrow = lax.broadcasted_iota(jnp.int32, (ss, ss), 0)
        kcol = lax.broadcasted_iota(jnp.int32, (ss, ss), 1)
        sc[steps] = jnp.full((ss, w), INT_MIN, jnp.int32)
        sc[steps, :, 0:ss] = _sort_key(s_new, kcol <= qrow)

        def count_ge(t):
            return jnp.sum(jnp.sum(jnp.where(sc[...] >= t[None], 1.0, 0.0), axis=0), axis=-1, keepdims=True)

        thr = _kth_largest(count_ge, (ss, 1), float(kk))
        n_gt = jnp.sum(jnp.sum(jnp.where(sc[...] > thr[None], 1.0, 0.0), axis=0), axis=-1, keepdims=True)
        need = float(kk) - n_gt
        excess = (count_ge(thr) - n_gt > need) & (thr > INT_MIN)

        @pl.when(jnp.max(jnp.where(excess, 1.0, 0.0)) > 0.0)
        def _():
            def slot(s, carry):
                for c in range(w // LANE):
                    keys = sc[s, :, c * LANE:(c + 1) * LANE]
                    tie = keys == thr
                    drop = excess & tie & (carry + _tie_prefix(tie) >= need)
                    sc[s, :, c * LANE:(c + 1) * LANE] = jnp.where(drop, jnp.int32(INT_MIN), keys)
                    carry = carry + jnp.sum(jnp.where(tie, 1.0, 0.0), axis=-1, keepdims=True)
                return carry
            lax.fori_loop(0, steps + 1, slot, jnp.zeros((ss, 1), F32))

        thr2 = jnp.maximum(thr, INT_MIN + 1)
        mask_ref[...] = jnp.where(sc[...] >= thr2[None], 0.0, NEG)


def _dsa_sa_body(pt_ref, cq_ref, new_ref, mask_ref, mask_new_ref, *rest, g_pages, steps, ss):
    pages = rest[:g_pages]
    o_ref, qs, m_ref, l_ref, acc_ref = rest[g_pages:]
    nh = C_HEADS
    step = pl.program_id(1)

    @pl.when(step == 0)
    def _():
        for h in range(nh):
            qs[h * ss:(h + 1) * ss, :] = cq_ref[:, h * C_DIM:(h + 1) * C_DIM]
        _softmax_init(m_ref, l_ref, acc_ref)

    q = qs[...].astype(BF)
    k_t = _cat_pages_t(pages, 0, C_DIM).astype(BF)
    v_t = _cat_pages_t(pages, C_DIM, 2 * C_DIM).astype(BF)
    _softmax_step(_add_head_mask(_dot(q, k_t) * C_SCALE, mask_ref[...], nh), v_t, m_ref, l_ref, acc_ref,
                  v_is_transposed=True)

    @pl.when(step == steps - 1)
    def _():
        kn = new_ref[:, 0:C_DIM].astype(BF)
        vn = new_ref[:, C_DIM:2 * C_DIM].astype(BF)
        s_new = _add_head_mask(_dot_nt(q, kn) * C_SCALE, mask_new_ref[:, 0:ss], nh)
        _softmax_step(s_new, vn, m_ref, l_ref, acc_ref)
        res = _softmax_finish(l_ref, acc_ref)
        for h in range(nh):
            o_ref[:, h * C_DIM:(h + 1) * C_DIM] = res[h * ss:(h + 1) * ss, :].astype(o_ref.dtype)


def dsa_sample(cq, iq, iw, c_rows, pool, page_table, tp, ns, ss):
    n_pages = page_table.shape[1]
    g_pages = _pages_per_step(n_pages)
    steps = n_pages // g_pages
    w = g_pages * PAGE_SIZE
    row_w = 2 * C_DIM + C_IDX_DIM
    kk = min(C_TOPK, (n_pages * PAGE_SIZE + ss) // 4)
    tok = lambda wd: pl.BlockSpec((ss, wd), lambda b, s, pt: (tp // ss + b, 0))
    pool_t = jnp.swapaxes(pool, 1, 2)
    mask = pl.pallas_call(
        functools.partial(_dsa_si_body, g_pages=g_pages, steps=steps, ss=ss, kk=kk),
        grid_spec=pltpu.PrefetchScalarGridSpec(
            num_scalar_prefetch=1, grid=(ns, steps),
            in_specs=[tok(C_IDX_HEADS * LANE), tok(LANE), tok(row_w)]
            + _page_specs_t(g_pages, C_IDX_DIM, 2 * C_DIM // C_IDX_DIM),
            out_specs=pl.BlockSpec((None, steps + 1, ss, w), lambda b, s, pt: (b, 0, 0, 0)),
            scratch_shapes=[pltpu.VMEM((C_IDX_HEADS * ss, LANE), F32), pltpu.VMEM((steps + 1, ss, w), jnp.int32)]),
        out_shape=jax.ShapeDtypeStruct((ns, steps + 1, ss, w), F32),
        compiler_params=_cparams("parallel", "arbitrary"), name="dsa_sample_index",
    )(page_table, iq, iw, c_rows, *([pool_t] * g_pages))
    m = C_HEADS * ss
    return pl.pallas_call(
        functools.partial(_dsa_sa_body, g_pages=g_pages, steps=steps, ss=ss),
        grid_spec=pltpu.PrefetchScalarGridSpec(
            num_scalar_prefetch=1, grid=(ns, steps),
            in_specs=[tok(C_HEADS * C_DIM), tok(row_w),
                      pl.BlockSpec((None, None, ss, w), lambda b, s, pt: (b, s, 0, 0)),
                      pl.BlockSpec((None, None, ss, w), lambda b, s, pt: (b, steps, 0, 0))]
            + _page_specs_t(g_pages, 2 * C_DIM, 0),
            out_specs=pl.BlockSpec((ss, C_HEADS * C_DIM), lambda b, s, pt: (b, 0)),
            scratch_shapes=[pltpu.VMEM((m, C_DIM), F32), pltpu.VMEM((m, 1), F32), pltpu.VMEM((m, LANE), F32),
                            pltpu.VMEM((m, C_DIM), F32)]),
        out_shape=jax.ShapeDtypeStruct((ns * ss, C_HEADS * C_DIM), BF),
        compiler_params=_cparams("parallel", "arbitrary"), name="dsa_sample_attend",
    )(page_table, cq, c_rows, mask, mask, *([pool_t] * g_pages))


_CH = D_CMP_STRIDE
_NSA_HEADS_PER_CHUNK = 4
_SLC_PER_CH = D_SLC_BLOCK // D_CMP_STRIDE


def _round_up(n, m):
    return -(-n // m) * m


def _phi_weights(pe, w1, w2):
    nch = D_CMP_LEN // _CH
    halves = []
    for half in range(nch):
        wk = w1[0, half * _CH * D_DIM:(half + 1) * _CH * D_DIM].reshape(_CH, D_DIM, -1)
        wv = w1[1, half * _CH * D_DIM:(half + 1) * _CH * D_DIM].reshape(_CH, D_DIM, -1)
        z = jnp.zeros_like(wk)
        blk = jnp.concatenate([jnp.concatenate([wk, z], axis=2), jnp.concatenate([z, wv], axis=2)], axis=1)
        halves.append(blk.reshape(_CH * 2 * D_DIM, -1).astype(BF))
    pes = [jnp.concatenate([pe[0, half * _CH:(half + 1) * _CH], pe[1, half * _CH:(half + 1) * _CH]],
                           axis=1).reshape(1, -1) for half in range(nch)]
    z2 = jnp.zeros_like(w2[0])
    w2b = jnp.concatenate([jnp.concatenate([w2[0], z2], axis=1), jnp.concatenate([z2, w2[1]], axis=1)], axis=0)
    return jnp.concatenate(halves, axis=1), pes[0], pes[1], w2b.astype(BF)


def _chunk_rows(read, n_chunks):
    return jnp.concatenate([read(j, n_chunks) for j in range(_CH)], axis=1)


def _compress_pre(x, wcat, pelo, pehi):
    y = _dot(x.astype(BF), wcat)
    pe = jnp.concatenate([jnp.broadcast_to(pelo, (4, pelo.shape[1])), jnp.broadcast_to(pehi, (4, pehi.shape[1]))], axis=0)
    bias = _dot(pe.astype(BF), wcat)
    lo = y[:, 0:LANE] + bias[0:1, 0:LANE]
    hi = y[:, LANE:2 * LANE] + bias[4:5, LANE:2 * LANE]
    return lo, hi


def _compress_post(lo, hi, w2b):
    n = lo.shape[0]
    return _dot(jax.nn.silu(lo + pltpu.roll(hi, n - 1, 0)).astype(BF), w2b)


def _cmp_branch(q, kvc, qpos, n_tok, nh, tq):
    ncp = kvc.shape[0]
    c_idx = lax.broadcasted_iota(jnp.int32, (tq, ncp), 1)
    vis = (c_idx < n_tok) & (c_idx * D_CMP_STRIDE + (D_CMP_LEN - 1) <= qpos)
    maskf = jnp.where(vis, 0.0, NEG)
    hpc = min(nh, max(1, 512 // tq))
    outs, psum = [], None
    for c0 in range(0, nh, hpc):
        s = _add_head_mask(_dot_nt(q[c0 * tq:(c0 + hpc) * tq], kvc), maskf, hpc)
        m = jnp.maximum(jnp.max(s, axis=-1, keepdims=True), M_FLOOR)
        p = jnp.exp(s - m)
        l = jnp.sum(p, axis=-1, keepdims=True)
        l = jnp.where(l > 0.0, l, 1.0)
        outs.append(_dot(p.astype(BF), kvc) / l)
        pn = p / l
        for h in range(hpc):
            psum = pn[h * tq:(h + 1) * tq] if psum is None else psum + pn[h * tq:(h + 1) * tq]
    return jnp.concatenate(outs, axis=0), psum


def _block_importance(psum, nsbp):
    ncp = psum.shape[1]
    c = lax.broadcasted_iota(jnp.int32, (ncp, nsbp), 0)
    j = lax.broadcasted_iota(jnp.int32, (ncp, nsbp), 1)
    rc = D_CMP_LEN // D_CMP_STRIDE
    band = jnp.where((c >= _SLC_PER_CH * j - (rc - 1)) & (c <= _SLC_PER_CH * j + _SLC_PER_CH - 1), 1.0, 0.0).astype(BF)
    hi = psum.astype(BF)
    r1 = psum - hi.astype(F32)
    mid = r1.astype(BF)
    lo = (r1 - mid.astype(F32)).astype(BF)
    return _dot(hi, band) + _dot(mid, band) + _dot(lo, band)


def _select_blocks(imp, qpos_col, nsb):
    jb = lax.broadcasted_iota(jnp.int32, imp.shape, 1)
    bt = qpos_col // D_SLC_BLOCK
    forced = (jb == 0) | (jb >= bt - (D_SLC_LOCAL - 1))
    val = jnp.where(forced, jnp.inf, imp)
    sel = _top_lanes(val, (jb <= bt) & (jb < nsb), min(D_SLC_TOPN, nsb))
    return jnp.where(sel, 1.0, 0.0)


def _slc_mask(selm, k0, tk, qpos):
    nsbp = selm.shape[1]
    jb = lax.broadcasted_iota(jnp.int32, (nsbp, tk), 0)
    kk = lax.broadcasted_iota(jnp.int32, (nsbp, tk), 1)
    expand = jnp.where((k0 + kk) // D_SLC_BLOCK == jb, 1.0, 0.0).astype(BF)
    chosen = _dot(selm.astype(BF), expand) > 0.5
    kpos = k0 + lax.broadcasted_iota(jnp.int32, (selm.shape[0], tk), 1)
    return jnp.where(chosen & (kpos <= qpos), 0.0, NEG)


def _nsa_combine(o_cmp, o_slc, o_win, gates, nh, tq, o_ref):
    lane = lax.broadcasted_iota(jnp.int32, (tq, LANE), 1)
    outs = []
    for h in range(nh):
        r = slice(h * tq, (h + 1) * tq)
        outs.append(gates[:, h:h + 1] * o_cmp[r] + gates[:, nh + h:nh + h + 1] * o_slc[r]
                    + gates[:, 2 * nh + h:2 * nh + h + 1] * o_win[r])
    for j in range(nh // 2):
        o_ref[:, j * LANE:(j + 1) * LANE] = jnp.where(lane < D_DIM, pltpu.roll(outs[2 * j], D_DIM, 1),
                                                      outs[2 * j + 1]).astype(o_ref.dtype)


def _nsa_p_body(dq_ref, gate_ref, cmp_ref, slc_ref, win_ref, wcat_ref, pelo_ref, pehi_ref, w2_ref, o_ref,
                qs, kvc, m_ref, l_ref, acc_ref, m2_ref, l2_ref, acc2_ref, *, tq, tk, sp):
    nh = D_HEADS
    qi = pl.program_id(1)
    q0 = qi * tq
    n_ch = sp // _CH
    ncp = kvc.shape[0]
    nsb = sp // D_SLC_BLOCK
    nsbp = _round_up(nsb, LANE)

    @pl.when(qi == 0)
    def _():
        x = _chunk_rows(lambda j, n: cmp_ref[pl.ds(j, n, stride=_CH), :], n_ch)
        lo, hi = _compress_pre(x, wcat_ref[...], pelo_ref[...], pehi_ref[...])
        kvc[...] = jnp.zeros(kvc.shape, kvc.dtype)
        kvc[0:n_ch, :] = _compress_post(lo, hi, w2_ref[...]).astype(kvc.dtype)

    for h in range(nh):
        qs[h * tq:(h + 1) * tq, :] = dq_ref[:, h * LANE:(h + 1) * LANE].astype(BF)
    q = qs[...]
    qpos_col = q0 + lax.broadcasted_iota(jnp.int32, (tq, 1), 0)
    o_cmp, psum = _cmp_branch(q, kvc[...], qpos_col, n_ch - 1, nh, tq)
    selm = _select_blocks(_block_importance(psum, nsbp), qpos_col, nsb)
    _softmax_init(m_ref, l_ref, acc_ref)
    for j in range(sp // tk):
        @pl.when(j * tk < q0 + tq)
        def _():
            kv = slc_ref[j * tk:(j + 1) * tk, :].astype(BF)
            maskf = _slc_mask(selm, j * tk, tk, qpos_col)
            _attend_tile([qs], lambda qc: _dot_nt(qc, kv), kv, m_ref, l_ref, acc_ref,
                         tq=tq, heads_per_chunk=_NSA_HEADS_PER_CHUNK, maskf=maskf)
    o_slc = _softmax_finish(l_ref, acc_ref)
    _softmax_init(m2_ref, l2_ref, acc2_ref)
    base = (q0 // tk) * tk
    for w in range(D_WINDOW // tk + 1):
        k0 = base - D_WINDOW + w * tk

        @pl.when(k0 >= 0)
        def _():
            kv = win_ref[pl.ds(pl.multiple_of(k0, tk), tk), :].astype(BF)
            dist = qpos_col - (k0 + lax.broadcasted_iota(jnp.int32, (tq, tk), 1))
            maskf = jnp.where((dist >= 0) & (dist < D_WINDOW), 0.0, NEG)
            _attend_tile([qs], lambda qc: _dot_nt(qc, kv), kv, m2_ref, l2_ref, acc2_ref,
                         tq=tq, heads_per_chunk=_NSA_HEADS_PER_CHUNK, maskf=maskf)
    o_win = _softmax_finish(l2_ref, acc2_ref)
    _nsa_combine(o_cmp, o_slc, o_win, gate_ref[...], nh, tq, o_ref)


def nsa_prompt(dq, gates, d_rows, win_rows, phi, nb, sp):
    tq = _pick_tile(sp, 128, 8)
    tk = _pick_tile(sp, 256, LANE)
    assert D_WINDOW % tk == 0 and tk % tq == 0
    nq = sp // tq
    m = D_HEADS * tq
    ncp = _round_up(sp // _CH, LANE)
    tok = lambda w: pl.BlockSpec((tq, w), lambda b, i: (b * nq + i, 0))
    seq = lambda col: pl.BlockSpec((sp, LANE), lambda b, i: (b, col))
    stats = [pltpu.VMEM((m, LANE), F32), pltpu.VMEM((m, LANE), F32), pltpu.VMEM((m, LANE), F32)]
    return pl.pallas_call(
        functools.partial(_nsa_p_body, tq=tq, tk=tk, sp=sp), grid=(nb, nq),
        in_specs=[tok(D_HEADS * LANE), tok(LANE), seq(0), seq(1), seq(0)] + [_full_spec(a.shape) for a in phi],
        out_specs=tok(D_HEADS * D_DIM),
        out_shape=jax.ShapeDtypeStruct((nb * sp, D_HEADS * D_DIM), BF),
        scratch_shapes=[pltpu.VMEM((m, LANE), BF), pltpu.VMEM((ncp, LANE), BF)] + stats + stats,
        compiler_params=_cparams("parallel", "arbitrary"), name="nsa_prompt",
    )(dq, gates, d_rows, d_rows, win_rows, *phi)


def _nsa_sc_body(pt_ref, dq_ref, wcat_ref, pelo_ref, pehi_ref, w2_ref, *rest,
                 g_pages, steps, ss, past_len, nsb):
    pages = rest[:g_pages]
    ocmp_ref, selm_ref, qs, los, his = rest[g_pages:]
    nh = D_HEADS
    step = pl.program_id(1)
    cpp = PAGE_SIZE // _CH
    n_ch = past_len // _CH

    @pl.when(step == 0)
    def _():
        for h in range(nh):
            qs[h * ss:(h + 1) * ss, :] = dq_ref[:, h * LANE:(h + 1) * LANE]

    x = jnp.concatenate([_chunk_rows(lambda j, n: p[pl.ds(j, n, stride=_CH), :], cpp) for p in pages], axis=0)
    lo, hi = _compress_pre(x, wcat_ref[...], pelo_ref[...], pehi_ref[...])
    r0 = pl.multiple_of(step * (g_pages * cpp), g_pages * cpp)
    los[pl.ds(r0, g_pages * cpp), :] = lo
    his[pl.ds(r0, g_pages * cpp), :] = hi

    @pl.when(step == steps - 1)
    def _():
        kvc = _compress_post(los[...], his[...], w2_ref[...]).astype(BF)
        qpos_col = past_len + lax.broadcasted_iota(jnp.int32, (ss, 1), 0)
        o_cmp, psum = _cmp_branch(qs[...].astype(BF), kvc, qpos_col, n_ch - 1, nh, ss)
        ocmp_ref[...] = o_cmp
        selm_ref[...] = _select_blocks(_block_importance(psum, selm_ref.shape[-1]), qpos_col, nsb)


def _nsa_sa_body(pt_ref, dq_ref, gate_ref, new_ref, wnew_ref, wstate_ref, ocmp_ref, selm_ref, *rest,
                 g_pages, steps, ss, past_len):
    pages = rest[:g_pages]
    o_ref, qs, m_ref, l_ref, acc_ref = rest[g_pages:]
    nh = D_HEADS
    step = pl.program_id(1)
    w = g_pages * PAGE_SIZE

    @pl.when(step == 0)
    def _():
        for h in range(nh):
            qs[h * ss:(h + 1) * ss, :] = dq_ref[:, h * LANE:(h + 1) * LANE]
        _softmax_init(m_ref, l_ref, acc_ref)

    q = qs[...].astype(BF)
    qpos_col = past_len + lax.broadcasted_iota(jnp.int32, (ss, 1), 0)
    selm = selm_ref[...]
    kv = jnp.concatenate([p[...] for p in pages], axis=0).astype(BF)
    maskf = _slc_mask(selm, step * w, w, qpos_col)
    _softmax_step(_add_head_mask(_dot_nt(q, kv), maskf, nh), kv, m_ref, l_ref, acc_ref)

    @pl.when(step == steps - 1)
    def _():
        kvn = new_ref[:, LANE:2 * LANE].astype(BF)
        maskn = _slc_mask(selm, past_len, ss, qpos_col)
        _softmax_step(_add_head_mask(_dot_nt(q, kvn), maskn, nh), kvn, m_ref, l_ref, acc_ref)
        o_slc = _softmax_finish(l_ref, acc_ref)
        _softmax_init(m_ref, l_ref, acc_ref)
        wbuf = wstate_ref.shape[0]
        for kvw, k0 in ((wstate_ref[...].astype(BF), past_len - wbuf), (wnew_ref[...].astype(BF), past_len)):
            n = kvw.shape[0]
            dist = qpos_col - (k0 + lax.broadcasted_iota(jnp.int32, (ss, n), 1))
            maskw = jnp.where((dist >= 0) & (dist < D_WINDOW), 0.0, NEG)
            _softmax_step(_add_head_mask(_dot_nt(q, kvw), maskw, nh), kvw, m_ref, l_ref, acc_ref)
        o_win = _softmax_finish(l_ref, acc_ref)
        _nsa_combine(ocmp_ref[...], o_slc, o_win, gate_ref[...], nh, ss, o_ref)


def nsa_sample(dq, gates, d_rows, win_rows, win_state, phi, pool, page_table, tp, ns, ss):
    n_pages = page_table.shape[1]
    g_pages = _pages_per_step(n_pages)
    steps = n_pages // g_pages
    past_len = n_pages * PAGE_SIZE
    assert ss <= D_CMP_STRIDE and past_len % D_SLC_BLOCK == 0
    n_ch = past_len // _CH
    nsb = -(-(past_len + ss) // D_SLC_BLOCK)
    nsbp = _round_up(nsb, LANE)
    m = D_HEADS * ss
    tok = lambda wd: pl.BlockSpec((ss, wd), lambda b, s, pt: (tp // ss + b, 0))
    per_seq = lambda shape: pl.BlockSpec((None,) + shape, lambda b, s, pt: (b,) + (0,) * len(shape))
    phi_specs = [pl.BlockSpec(a.shape, lambda b, s, pt, nd=a.ndim: (0,) * nd) for a in phi]
    gc_pages = _pages_per_step(n_pages, 32)
    o_cmp, selm = pl.pallas_call(
        functools.partial(_nsa_sc_body, g_pages=gc_pages, steps=n_pages // gc_pages, ss=ss, past_len=past_len,
                          nsb=nsb),
        grid_spec=pltpu.PrefetchScalarGridSpec(
            num_scalar_prefetch=1, grid=(ns, n_pages // gc_pages),
            in_specs=[tok(D_HEADS * LANE)] + phi_specs + _page_specs(gc_pages, LANE, 0),
            out_specs=[per_seq((m, LANE)), per_seq((ss, nsbp))],
            scratch_shapes=[pltpu.VMEM((m, LANE), F32), pltpu.VMEM((n_ch, LANE), F32), pltpu.VMEM((n_ch, LANE), F32)]),
        out_shape=[jax.ShapeDtypeStruct((ns, m, LANE), F32), jax.ShapeDtypeStruct((ns, ss, nsbp), F32)],
        compiler_params=_cparams("parallel", "arbitrary"), name="nsa_sample_compress",
    )(page_table, dq, *phi, *([pool] * gc_pages))
    wbuf = win_state.shape[1]
    return pl.pallas_call(
        functools.partial(_nsa_sa_body, g_pages=g_pages, steps=steps, ss=ss, past_len=past_len),
        grid_spec=pltpu.PrefetchScalarGridSpec(
            num_scalar_prefetch=1, grid=(ns, steps),
            in_specs=[tok(D_HEADS * LANE), tok(LANE), tok(4 * D_DIM), tok(LANE), per_seq((wbuf, LANE)),
                      per_seq((m, LANE)), per_seq((ss, nsbp))] + _page_specs(g_pages, LANE, 1),
            out_specs=pl.BlockSpec((ss, D_HEADS * D_DIM), lambda b, s, pt: (b, 0)),
            scratch_shapes=[pltpu.VMEM((m, LANE), F32), pltpu.VMEM((m, 1), F32), pltpu.VMEM((m, LANE), F32),
                            pltpu.VMEM((m, LANE), F32)]),
        out_shape=jax.ShapeDtypeStruct((ns * ss, D_HEADS * D_DIM), BF),
        compiler_params=_cparams("parallel", "arbitrary"), name="nsa_sample_attend",
    )(page_table, dq, gates, d_rows, win_rows, win_state, o_cmp, selm, *([pool] * g_pages))


def odd_mixers(h, pos, tp, nb, sp, ns, ss, pool_c, pool_d, page_table, win_state, phi_pe, phi_w1, phi_w2):
    tab_a = _rope_tables(pos, C_DIM // 4, ROPE_THETA, C_DIM)
    tab_i = _rope_tables(pos, C_IDX_DIM // 4, ROPE_THETA, C_IDX_DIM)
    cq, c_rows, iq, iw, dq, d_rows, win_rows, gates = prep_odd(h, tab_a, tab_i)
    phi = _phi_weights(phi_pe, phi_w1, phi_w2)
    o_c = jnp.concatenate([dsa_prompt(cq, iq, iw, c_rows, nb, sp),
                           dsa_sample(cq, iq, iw, c_rows, pool_c, page_table, tp, ns, ss)], axis=0)
    o_d = jnp.concatenate([nsa_prompt(dq, gates, d_rows, win_rows, phi, nb, sp),
                           nsa_sample(dq, gates, d_rows, win_rows, win_state, phi, pool_d, page_table, tp, ns, ss)],
                          axis=0)
    return jnp.concatenate([o_c, o_d], axis=1), c_rows, d_rows, win_rows


def _pad_cols(w, mult):
    n = w.shape[1]
    npad = -(-n // mult) * mult
    return jnp.pad(w, ((0, 0), (0, npad - n))) if npad != n else w


def kernel(x_prompt, x_sample, cache_a_kv, cache_b_latent, cache_c_kvi, cache_d_kv, state_d_win, page_table,
           w_in_even, b_g_cq, b_g_ckv, b_w_uq, b_w_ukv, w_out_even,
           w_in_odd, d_phi_pe, d_phi_w1, d_phi_w2, w_out_odd,
           mlp_w1, mlp_w2, ln_g, ln_b):
    nb, sp, d = x_prompt.shape
    ns, ss, _ = x_sample.shape
    tp = nb * sp
    past_len = page_table.shape[1] * PAGE_SIZE
    pos_p = jnp.arange(sp, dtype=jnp.int32)
    pos_s = past_len + jnp.arange(ss, dtype=jnp.int32)
    bf = jnp.bfloat16
    x = jnp.concatenate([x_prompt.reshape(tp, d), x_sample.reshape(ns * ss, d)], axis=0)

    def groups(h):
        return h[:tp].reshape(nb, sp, -1), h[tp:].reshape(ns, ss, -1)

    def post(x, o, w_out, layer):
        x, xb = resid_layer_norm(x, matmul(o, w_out.astype(bf)), ln_g[layer, 0], ln_b[layer, 0])
        hm = matmul(xb, mlp_w1[layer].astype(bf), act="relu2", out_dtype=bf)
        return resid_layer_norm(x, matmul(hm, mlp_w2[layer].astype(bf)), ln_g[layer, 1], ln_b[layer, 1])

    pos_all = jnp.concatenate([jnp.tile(pos_p, nb), jnp.tile(pos_s, ns)])
    h0 = matmul(x, _pad_cols(w_in_even, LANE).astype(bf))
    o0, a_rows, b_rows = even_mixers(h0, pos_all, tp, nb, sp, ns, ss, cache_a_kv, cache_b_latent, page_table,
                                     b_g_cq, b_g_ckv, b_w_uq, b_w_ukv)
    a_p, a_s = groups(a_rows)
    b_p, b_s = groups(b_rows)
    x, xb = post(x, o0, w_out_even, 0)
    h1 = matmul(xb, _odd_weight_cols(w_in_odd).astype(bf))
    o1, c_rows, d_rows, win_rows = odd_mixers(h1, pos_all, tp, nb, sp, ns, ss, cache_c_kvi, cache_d_kv, page_table,
                                              state_d_win, d_phi_pe, d_phi_w1, d_phi_w2)
    c_p, c_s = groups(c_rows)
    d_p, d_s = groups(d_rows)
    win_p, win_s = groups(win_rows)
    w_p = win_p[:, -min(D_WINDOW, sp):]
    w_s = jnp.concatenate([state_d_win, win_s], axis=1)[:, -state_d_win.shape[1]:]
    x, _ = post(x, o1, w_out_odd, 1)
    y_p, y_s = groups(x)
    return (y_p, y_s, a_p, a_s, b_p, b_s, c_p, c_s, d_p, d_s, w_p, w_s)
```
